```python
import math
import jax
import jax.numpy as jnp
from jax import lax
import numpy as np

D_MODEL = 4096
BATCH = 4
SEQ = 4096
DEPTH = 2

HEAD_DIM = 128
N_MIX_HEADS = D_MODEL // HEAD_DIM
SWA_HEADS = 3 * N_MIX_HEADS // 8
SWA_KV_HEADS = SWA_HEADS // 3
MLSTM_HEADS = N_MIX_HEADS // 4
DIFF_HEADS = N_MIX_HEADS - SWA_HEADS - MLSTM_HEADS
DIFF_QK_DIM = HEAD_DIM // 2
SWA_Q_W = SWA_HEADS * HEAD_DIM
SWA_KV_W = SWA_KV_HEADS * HEAD_DIM
MLSTM_W = MLSTM_HEADS * HEAD_DIM
MLSTM_GATE_W = 4 * MLSTM_HEADS
DIFF_W = DIFF_HEADS * HEAD_DIM
IN_PROJ_W = SWA_Q_W + 2 * SWA_KV_W + 4 * MLSTM_W + MLSTM_GATE_W + 3 * DIFF_W
WINDOW = 128
QUERY_BLOCK = 128
MLSTM_CHUNK = 128
CONV_WIDTH = 5
ROPE_THETA = 10000.0
N_EXPERTS = 48
TOP_K = 6
N_GROUPS = 8
TOPK_GROUPS = 4
EXPERT_DIM = D_MODEL // 8
SHARED_DIM = EXPERT_DIM
ROUTED_SCALE = 2.5
DISPATCH_BLOCK = 256
ADA_CHUNKS = 6
EPS = 1e-6

kernel_name = 'hybrid_swa_mlstm_diffattn_moe_encoder'


def rms_norm(x, g):
    xf = x.astype(jnp.float32)
    y = xf * lax.rsqrt(jnp.mean(xf * xf, axis=-1, keepdims=True) + EPS)
    return (y * g.astype(jnp.float32)).astype(x.dtype)


def rope(x, pos):
    S, d = x.shape[1], x.shape[-1]
    half = d // 2
    inv = jnp.power(ROPE_THETA, -jnp.arange(half, dtype=jnp.float32) / half)
    ang = pos.astype(jnp.float32)[:, None] * inv[None, :]
    shape = (1, S) + (1,) * (x.ndim - 3) + (half,)
    cos = jnp.cos(ang).reshape(shape).astype(x.dtype)
    sin = jnp.sin(ang).reshape(shape).astype(x.dtype)
    x1, x2 = x[..., :half], x[..., half:]
    return jnp.concatenate([x1 * cos - x2 * sin, x1 * sin + x2 * cos], axis=-1)


def swa_attention(q, k, v, sink):
    B, S, Hq, d = q.shape
    Hkv = k.shape[2]
    G = Hq // Hkv
    W = WINDOW
    nb = S // W
    qb = q.reshape(B, nb, W, Hkv, G, d)

    def band(t):
        tp = jnp.pad(t, ((0, 0), (W, W), (0, 0), (0, 0))).reshape(B, nb + 2, W, Hkv, d)
        return jnp.concatenate([tp[:, :-2], tp[:, 1:-1], tp[:, 2:]], axis=2)

    kb, vb = band(k), band(v)
    s = jnp.einsum('bnqhgd,bnkhd->bnhgqk', qb, kb).astype(jnp.float32) * (d ** -0.5)
    qpos = jnp.arange(nb)[:, None] * W + jnp.arange(W)[None, :]
    kpos = (jnp.arange(nb)[:, None] - 1) * W + jnp.arange(3 * W)[None, :]
    rel = kpos[:, None, :] - qpos[:, :, None]
    valid = (jnp.abs(rel) <= WINDOW) & (kpos >= 0)[:, None, :] & (kpos < S)[:, None, :]
    s = jnp.where(valid[None, :, None, None, :, :], s, -jnp.inf)
    sk = sink.astype(jnp.float32).reshape(Hkv, G)[None, None, :, :, None, None]
    mx = jnp.maximum(s.max(-1, keepdims=True), sk)
    p = jnp.exp(s - mx)
    p = p / (p.sum(-1, keepdims=True) + jnp.exp(sk - mx))
    o = jnp.einsum('bnhgqk,bnkhd->bnqhgd', p.astype(v.dtype), vb)
    return o.reshape(B, S, Hq * d)


def centred_dwconv(x, w, b):
    C = x.shape[-1]
    out = lax.conv_general_dilated(x, w[:, None, :].astype(x.dtype), window_strides=(1,), padding='SAME',
                                   dimension_numbers=('NWC', 'WIO', 'NWC'), feature_group_count=C)
    return out + b.astype(x.dtype)


def mlstm_scan(q, k, v, log_i, log_f):
    B, H, S, d = q.shape
    L = MLSTM_CHUNK
    nc = S // L

    def chunks(t):
        return jnp.moveaxis(t.reshape(t.shape[:2] + (nc, L) + t.shape[3:]), 2, 0)

    qc, kc, vc, ic = chunks(q), chunks(k), chunks(v), chunks(log_i)
    bc = jnp.cumsum(chunks(log_f), axis=-1)
    tril = jnp.tril(jnp.ones((L, L), dtype=bool))

    def step(carry, inp):
        C, n, m = carry
        qt, kt, vt, it, bt = inp
        log_w = jnp.where(tril, bt[..., :, None] - bt[..., None, :] + it[..., None, :], -jnp.inf)
        log_inter = bt + m[..., None]
        m_t = jnp.maximum(log_inter, log_w.max(-1))
        w_intra = jnp.exp(log_w - m_t[..., None])
        w_inter = jnp.exp(log_inter - m_t)
        s = jnp.einsum('bhtd,bhsd->bhts', qt, kt) * w_intra
        num = w_inter[..., None] * jnp.einsum('bhtd,bhde->bhte', qt, C) + jnp.einsum('bhts,bhse->bhte', s, vt)
        den = w_inter * jnp.einsum('bhtd,bhd->bht', qt, n) + s.sum(-1)
        h = num / jnp.maximum(jnp.abs(den), jnp.exp(-m_t))[..., None]
        b_last = bt[..., -1]
        log_ws = b_last[..., None] - bt + it
        m_new = jnp.maximum(b_last + m, log_ws.max(-1))
        decay = jnp.exp(b_last + m - m_new)
        ws = jnp.exp(log_ws - m_new[..., None])
        C = decay[..., None, None] * C + jnp.einsum('bhsd,bhse->bhde', kt * ws[..., None], vt)
        n = decay[..., None] * n + jnp.einsum('bhs,bhsd->bhd', ws, kt)
        return (C, n, m_new), h

    init = (jnp.zeros((B, H, d, d), jnp.float32), jnp.zeros((B, H, d), jnp.float32), jnp.zeros((B, H), jnp.float32))
    _, hs = lax.scan(step, init, (qc, kc, vc, ic, bc))
    return jnp.moveaxis(hs, 0, 2).reshape(B, H, S, d)


def mlstm_bidir(q_raw, k_raw, v_raw, o_pre, gates, conv_w, conv_b, gate_b, norm_g):
    B, S, _ = v_raw.shape
    H, d = MLSTM_HEADS, HEAD_DIM
    qk = jax.nn.silu(centred_dwconv(jnp.concatenate([q_raw, k_raw], axis=-1), conv_w, conv_b))
    q, k = jnp.split(qk, 2, axis=-1)

    def heads(t):
        return t.reshape(B, S, H, d).transpose(0, 2, 1, 3).astype(jnp.float32)

    q, k, v = heads(q), heads(k) * (d ** -0.5), heads(v_raw)
    g = (gates.astype(jnp.float32) + gate_b.astype(jnp.float32)).reshape(B, S, 4, H).transpose(2, 0, 3, 1)
    flip = lambda t: jnp.flip(t, axis=2)
    h_f = mlstm_scan(q, k, v, g[0], jax.nn.log_sigmoid(g[1]))
    h_b = flip(mlstm_scan(flip(q), flip(k), flip(v), flip(g[2]), flip(jax.nn.log_sigmoid(g[3]))))
    h = (h_f + h_b).transpose(0, 2, 1, 3)
    h = h * lax.rsqrt(jnp.mean(h * h, axis=-1, keepdims=True) + EPS) * norm_g.astype(jnp.float32).reshape(H, d)
    h = h.reshape(B, S, H * d) * jax.nn.sigmoid(o_pre.astype(jnp.float32))
    return h.astype(v_raw.dtype)


def diff_attention(q, k, v, lam, norm_g, layer_idx):
    B, S, H, _, dqk = q.shape
    d = v.shape[-1]
    lam_init = 0.8 - 0.6 * math.exp(-0.3 * layer_idx)
    lf = lam.astype(jnp.float32)
    lam_full = jnp.exp(jnp.sum(lf[0] * lf[1])) - jnp.exp(jnp.sum(lf[2] * lf[3])) + lam_init
    nq = S // QUERY_BLOCK
    qb = jnp.moveaxis(q.reshape(B, nq, QUERY_BLOCK, H, 2, dqk), 1, 0)
    scale = dqk ** -0.5

    def block(qi):
        s = jnp.einsum('bqhmd,bkhmd->bhmqk', qi, k).astype(jnp.float32) * scale
        p = jax.nn.softmax(s, axis=-1)
        a = p[:, :, 0] - lam_full * p[:, :, 1]
        return jnp.einsum('bhqk,bkhd->bqhd', a.astype(v.dtype), v)

    o = jnp.moveaxis(lax.map(block, qb), 0, 1).reshape(B, S, H, d).astype(jnp.float32)
    o = o * lax.rsqrt(jnp.mean(o * o, axis=-1, keepdims=True) + EPS) * norm_g.astype(jnp.float32) * (1.0 - lam_init)
    return o.reshape(B, S, H * d).astype(v.dtype)


def hybrid_mixer(h, pos, w_in, sink, conv_w, conv_b, gate_b, mnorm_g, lam, dnorm_g, w_out, layer_idx):
    B, S, _ = h.shape
    sizes = [SWA_Q_W, SWA_KV_W, SWA_KV_W, MLSTM_W, MLSTM_W, MLSTM_W, MLSTM_W, MLSTM_GATE_W, 2 * DIFF_HEADS * DIFF_QK_DIM, 2 * DIFF_HEADS * DIFF_QK_DIM, DIFF_W]
    points = [int(p) for p in np.cumsum(sizes)[:-1]]
    proj = h @ w_in
    aq, ak, av, mq, mk, mv, mo, mg, dq, dk, dv = jnp.split(proj, points, axis=-1)
    aq = rope(aq.reshape(B, S, SWA_HEADS, HEAD_DIM), pos)
    ak = rope(ak.reshape(B, S, SWA_KV_HEADS, HEAD_DIM), pos)
    out_a = swa_attention(aq, ak, av.reshape(B, S, SWA_KV_HEADS, HEAD_DIM), sink)
    out_b = mlstm_bidir(mq, mk, mv, mo, mg, conv_w, conv_b, gate_b, mnorm_g)
    dq = rope(dq.reshape(B, S, DIFF_HEADS, 2, DIFF_QK_DIM), pos)
    dk = rope(dk.reshape(B, S, DIFF_HEADS, 2, DIFF_QK_DIM), pos)
    out_c = diff_attention(dq, dk, dv.reshape(B, S, DIFF_HEADS, HEAD_DIM), lam, dnorm_g, layer_idx)
    return jnp.concatenate([out_a, out_b, out_c], axis=-1) @ w_out


def moe_ffn(h, router_w, router_b, wg, wu, wd, sg, su, sd):
    B, S, D = h.shape
    T = B * S
    xf = h.reshape(T, D)
    scores = jax.nn.sigmoid((xf @ router_w).astype(jnp.float32))
    biased = scores + router_b.astype(jnp.float32)
    per_group = N_EXPERTS // N_GROUPS
    gscore = lax.top_k(biased.reshape(T, N_GROUPS, per_group), 2)[0].sum(-1)
    _, gidx = lax.top_k(gscore, TOPK_GROUPS)
    gmask = (gidx[:, :, None] == jnp.arange(N_GROUPS)[None, None, :]).any(1)
    emask = jnp.repeat(gmask, per_group, axis=1)
    _, eidx = lax.top_k(jnp.where(emask, biased, -jnp.inf), TOP_K)
    w = jnp.take_along_axis(scores, eidx, axis=1)
    w = w / w.sum(-1, keepdims=True) * ROUTED_SCALE
    A = T * TOP_K
    flat_e = eidx.reshape(A)
    flat_t = jnp.repeat(jnp.arange(T, dtype=jnp.int32), TOP_K)
    flat_w = w.reshape(A)
    order = jnp.argsort(flat_e)
    se, st, sw = flat_e[order], flat_t[order], flat_w[order]
    counts = jnp.zeros((N_EXPERTS,), jnp.int32).at[flat_e].add(1)
    padded = (counts + DISPATCH_BLOCK - 1) // DISPATCH_BLOCK * DISPATCH_BLOCK
    pad_end = jnp.cumsum(padded)
    pad_start = pad_end - padded
    cnt_start = jnp.cumsum(counts) - counts
    dest = pad_start[se] + jnp.arange(A, dtype=jnp.int32) - cnt_start[se]
    nblk = (A + DISPATCH_BLOCK - 1) // DISPATCH_BLOCK + N_EXPERTS
    rows = nblk * DISPATCH_BLOCK
    row_t = jnp.full((rows,), T, jnp.int32).at[dest].set(st)
    row_w = jnp.zeros((rows,), jnp.float32).at[dest].set(sw)
    blk_e = jnp.clip(jnp.searchsorted(pad_end, jnp.arange(nblk, dtype=jnp.int32) * DISPATCH_BLOCK, side='right'), 0, N_EXPERTS - 1)
    x_pad = jnp.concatenate([xf, jnp.zeros((1, D), xf.dtype)], axis=0)

    def step(out, inp):
        t, wt, e = inp
        xb = x_pad[t]
        hid = jax.nn.silu(xb @ wg[e]) * (xb @ wu[e])
        y = (hid @ wd[e]) * wt[:, None].astype(hid.dtype)
        return out.at[t].add(y.astype(out.dtype)), None

    out0 = jnp.zeros((T + 1, D), xf.dtype)
    routed, _ = lax.scan(step, out0, (row_t.reshape(nblk, DISPATCH_BLOCK), row_w.reshape(nblk, DISPATCH_BLOCK), blk_e))
    shared = (jax.nn.silu(xf @ sg) * (xf @ su)) @ sd
    return (routed[:T] + shared).reshape(B, S, D)


def setup_inputs(seed: int = 0) -> dict:
    key = jax.random.key(seed)
    ks = jax.random.split(key, 26)
    f32 = jnp.float32
    nrm = lambda k, shape, s: jax.random.normal(k, shape, f32) * s
    D, E, F = D_MODEL, N_EXPERTS, EXPERT_DIM
    gate_offset = jnp.array([0.0, 3.0, 0.0, 3.0], f32)[None, :, None]
    return {
        'x': nrm(ks[0], (BATCH, SEQ, D), 1.0),
        'c': nrm(ks[1], (BATCH, D), 1.0),
        'ada_w': nrm(ks[2], (D, ADA_CHUNKS * D), 0.5 * D ** -0.5),
        'ada_b': nrm(ks[3], (ADA_CHUNKS * D,), 0.02),
        'ada_table': nrm(ks[4], (DEPTH, ADA_CHUNKS, D), D ** -0.5),
        'norm1_g': 1.0 + nrm(ks[5], (DEPTH, D), 0.02),
        'w_in': nrm(ks[6], (DEPTH, D, IN_PROJ_W), D ** -0.5),
        'swa_sink': nrm(ks[7], (DEPTH, SWA_HEADS), 0.5),
        'mlstm_conv_w': nrm(ks[8], (DEPTH, CONV_WIDTH, 2 * MLSTM_W), CONV_WIDTH ** -0.5),
        'mlstm_conv_b': nrm(ks[9], (DEPTH, 2 * MLSTM_W), 0.02),
        'mlstm_gate_b': (nrm(ks[10], (DEPTH, 4, MLSTM_HEADS), 0.1) + gate_offset).reshape(DEPTH, MLSTM_GATE_W),
        'mlstm_norm_g': 1.0 + nrm(ks[11], (DEPTH, MLSTM_W), 0.02),
        'diff_lambda': nrm(ks[12], (DEPTH, 4, DIFF_QK_DIM), 0.1),
        'diff_norm_g': 1.0 + nrm(ks[13], (DEPTH, HEAD_DIM), 0.02),
        'w_out': nrm(ks[14], (DEPTH, D, D), D ** -0.5),
        'norm2_g': 1.0 + nrm(ks[15], (DEPTH, D), 0.02),
        'router_w': nrm(ks[16], (DEPTH, D, E), D ** -0.5),
        'router_b': nrm(ks[17], (DEPTH, E), 0.01),
        'exp_gate': nrm(ks[18], (DEPTH, E, D, F), D ** -0.5),
        'exp_up': nrm(ks[19], (DEPTH, E, D, F), D ** -0.5),
        'exp_down': nrm(ks[20], (DEPTH, E, F, D), F ** -0.5),
        'sh_gate': nrm(ks[21], (DEPTH, D, SHARED_DIM), D ** -0.5),
        'sh_up': nrm(ks[22], (DEPTH, D, SHARED_DIM), D ** -0.5),
        'sh_down': nrm(ks[23], (DEPTH, SHARED_DIM, D), SHARED_DIM ** -0.5),
        'final_g': 1.0 + nrm(ks[24], (D,), 0.02),
    }


def reference(x, c, ada_w, ada_b, ada_table, norm1_g, w_in, swa_sink, mlstm_conv_w, mlstm_conv_b, mlstm_gate_b,
              mlstm_norm_g, diff_lambda, diff_norm_g, w_out, norm2_g, router_w, router_b, exp_gate, exp_up, exp_down,
              sh_gate, sh_up, sh_down, final_g):
    B, S, D = x.shape
    pos = jnp.arange(S, dtype=jnp.int32)
    cond = (jax.nn.silu(c) @ ada_w + ada_b).reshape(B, ADA_CHUNKS, D)
    for l in range(DEPTH):
        mod = cond + ada_table[l][None]
        sh_a, sc_a, g_a, sh_f, sc_f, g_f = [mod[:, i, None, :] for i in range(ADA_CHUNKS)]
        h = rms_norm(x, norm1_g[l]) * (1.0 + sc_a) + sh_a
        mix = hybrid_mixer(h, pos, w_in[l], swa_sink[l], mlstm_conv_w[l], mlstm_conv_b[l], mlstm_gate_b[l],
                           mlstm_norm_g[l], diff_lambda[l], diff_norm_g[l], w_out[l], l)
        x = x + g_a * mix
        h = rms_norm(x, norm2_g[l]) * (1.0 + sc_f) + sh_f
        x = x + g_f * moe_ffn(h, router_w[l], router_b[l], exp_gate[l], exp_up[l], exp_down[l],
                              sh_gate[l], sh_up[l], sh_down[l])
    return rms_norm(x, final_g)
```

```python
import functools
import math

import jax
import jax.numpy as jnp
from jax import lax
from jax.experimental import pallas as pl
from jax.experimental.pallas import tpu as pltpu

F32 = jnp.float32
BF16 = jnp.bfloat16
I32 = jnp.int32

HEAD_DIM = 128
SWA_GROUP = 3
WINDOW = 128
MLSTM_CHUNK = 128
CONV_WIDTH = 5
ROPE_THETA = 10000.0
N_EXPERTS = 48
TOP_K = 6
N_GROUPS = 8
TOPK_GROUPS = 4
ROUTED_SCALE = 2.5
ADA_CHUNKS = 6
EPS = 1e-6

LANES = 128
BF16_SUBLANES = 16
VMEM_LIMIT_V7X = 56 * 1024 * 1024

NT_DIMS = (((1,), (1,)), ((), ()))
TN_DIMS = (((0,), (0,)), ((), ()))


def _params(*sem):
    return pltpu.CompilerParams(dimension_semantics=sem, vmem_limit_bytes=VMEM_LIMIT_V7X)


def _silu(x):
    return x * jax.nn.sigmoid(x)


def _mod_kernel(c_ref, w_ref, b_ref, tab_ref, o_ref):
    a = _silu(c_ref[...])
    cond = jnp.dot(a.astype(BF16), w_ref[...].astype(BF16), preferred_element_type=F32) + b_ref[...]
    for l in range(tab_ref.shape[0]):
        o_ref[l] = cond + tab_ref[l]


def _modulation(c, ada_w, ada_b, ada_table, tn=512):
    B, D = c.shape
    depth = ada_table.shape[0]
    N = ada_w.shape[1]
    rows = 8
    c_pad = jnp.pad(c, ((0, rows - B), (0, 0)))
    out = pl.pallas_call(
        _mod_kernel,
        grid=(N // tn,),
        in_specs=[
            pl.BlockSpec((rows, D), lambda j: (0, 0)),
            pl.BlockSpec((D, tn), lambda j: (0, j)),
            pl.BlockSpec((1, tn), lambda j: (0, j)),
            pl.BlockSpec((depth, 1, tn), lambda j: (0, 0, j)),
        ],
        out_specs=pl.BlockSpec((depth, rows, tn), lambda j: (0, 0, j)),
        out_shape=jax.ShapeDtypeStruct((depth, rows, N), F32),
        compiler_params=_params("arbitrary"),
        name="adaln_mod",
    )(c_pad, ada_w, ada_b.reshape(1, N), ada_table.reshape(depth, 1, N))
    return out[:, :B].reshape(depth, B, ADA_CHUNKS, D)


def _norm_kernel(x_ref, g_ref, sc_ref, sh_ref, ws_ref, *out_refs, emit_f32):
    x = x_ref[...]
    ms = jnp.mean(x * x, axis=-1, keepdims=True)
    h = x * lax.rsqrt(ms + EPS) * g_ref[...]
    h = h * (1.0 + sc_ref[0]) + sh_ref[0]
    out_refs[0][...] = h.astype(BF16)
    if emit_f32:
        out_refs[1][...] = h
    out_refs[-1][...] = jnp.dot(h, ws_ref[...], preferred_element_type=F32,
                                precision=lax.Precision.HIGHEST)


def _norm_mod(x, g, scale, shift, w_small, seq, emit_f32, tr=256):
    T, D = x.shape
    B = scale.shape[0]
    ns = w_small.shape[1]
    per_b = seq // tr
    row = lambda i: (i, 0)
    out_shape = [jax.ShapeDtypeStruct((T, D), BF16)]
    out_specs = [pl.BlockSpec((tr, D), row)]
    if emit_f32:
        out_shape.append(jax.ShapeDtypeStruct((T, D), F32))
        out_specs.append(pl.BlockSpec((tr, D), row))
    out_shape.append(jax.ShapeDtypeStruct((T, ns), F32))
    out_specs.append(pl.BlockSpec((tr, ns), row))
    return pl.pallas_call(
        functools.partial(_norm_kernel, emit_f32=emit_f32),
        grid=(T // tr,),
        in_specs=[
            pl.BlockSpec((tr, D), row),
            pl.BlockSpec((1, D), lambda i: (0, 0)),
            pl.BlockSpec((1, 1, D), lambda i: (i // per_b, 0, 0)),
            pl.BlockSpec((1, 1, D), lambda i: (i // per_b, 0, 0)),
            pl.BlockSpec((D, ns), lambda i: (0, 0)),
        ],
        out_specs=out_specs,
        out_shape=out_shape,
        compiler_params=_params("arbitrary"),
        name="rmsnorm_adaln",
    )(x, g.reshape(1, D), scale.reshape(B, 1, D), shift.reshape(B, 1, D), w_small)


def _final_norm_kernel(x_ref, g_ref, o_ref):
    x = x_ref[...]
    ms = jnp.mean(x * x, axis=-1, keepdims=True)
    o_ref[...] = x * lax.rsqrt(ms + EPS) * g_ref[...]


def _final_norm(x, g, tr=256):
    T, D = x.shape
    return pl.pallas_call(
        _final_norm_kernel,
        grid=(T // tr,),
        in_specs=[pl.BlockSpec((tr, D), lambda i: (i, 0)), pl.BlockSpec((1, D), lambda i: (0, 0))],
        out_specs=pl.BlockSpec((tr, D), lambda i: (i, 0)),
        out_shape=jax.ShapeDtypeStruct((T, D), F32),
        compiler_params=_params("arbitrary"),
        name="final_rmsnorm",
    )(x, g.reshape(1, D))


def _rope_tile(acc, cos, sin, half):
    n = acc.shape[1]
    lane = lax.broadcasted_iota(I32, acc.shape, 1)
    from_right = pltpu.roll(acc, n - half, 1)
    from_left = pltpu.roll(acc, half, 1)
    rot = jnp.where((lane & (2 * half - 1)) < half, from_right, from_left)
    return acc * cos + rot * sin


def _inproj_kernel(a_ref, w_ref, cos_a, sin_a, cos_c, sin_c, o_ref, *, tiles):
    n_aqk, c_q0, c_k0, c_v0, q_scale = tiles
    j = pl.program_id(1)
    acc = jnp.dot(a_ref[...], w_ref[...], preferred_element_type=F32)

    @pl.when(j < n_aqk)
    def _():
        o_ref[...] = _rope_tile(acc, cos_a[...], sin_a[...], HEAD_DIM // 2).astype(o_ref.dtype)

    @pl.when(jnp.logical_and(j >= c_q0, j < c_k0))
    def _():
        o_ref[...] = (_rope_tile(acc, cos_c[...], sin_c[...], HEAD_DIM // 4) * q_scale).astype(o_ref.dtype)

    @pl.when(jnp.logical_and(j >= c_k0, j < c_v0))
    def _():
        o_ref[...] = _rope_tile(acc, cos_c[...], sin_c[...], HEAD_DIM // 4).astype(o_ref.dtype)

    @pl.when(jnp.logical_or(jnp.logical_and(j >= n_aqk, j < c_q0), j >= c_v0))
    def _():
        o_ref[...] = acc.astype(o_ref.dtype)


def _in_projection(h, w, rope_tabs, seq, tiles, tm=512, tn=512):
    T, D = h.shape
    N = w.shape[1]
    per_b = seq // tm
    tab = pl.BlockSpec((tm, tn), lambda i, j: (i % per_b, 0))
    return pl.pallas_call(
        functools.partial(_inproj_kernel, tiles=tiles),
        grid=(T // tm, N // tn),
        in_specs=[
            pl.BlockSpec((tm, D), lambda i, j: (i, 0)),
            pl.BlockSpec((D, tn), lambda i, j: (0, j)),
            tab, tab, tab, tab,
        ],
        out_specs=pl.BlockSpec((tm, tn), lambda i, j: (i, j)),
        out_shape=jax.ShapeDtypeStruct((T, N), BF16),
        compiler_params=_params("arbitrary", "arbitrary"),
        name="in_projection_rope",
    )(h, w, *rope_tabs)


def _rope_tables(seq, half, width):
    inv = jnp.power(ROPE_THETA, -jnp.arange(half, dtype=F32) / half)
    ang = jnp.arange(seq, dtype=F32)[:, None] * inv[None, :]
    cos, sin = jnp.cos(ang), jnp.sin(ang)
    reps = width // (2 * half)
    return (jnp.tile(jnp.concatenate([cos, cos], axis=1), (1, reps)),
            jnp.tile(jnp.concatenate([-sin, sin], axis=1), (1, reps)))


def _swa_kernel(sink_ref, q_ref, kp_ref, km_ref, kn_ref, vp_ref, vm_ref, vn_ref, o_ref, *, tq, seq):
    hk = pl.program_id(1)
    i = pl.program_id(2)
    W, G, d = WINDOW, SWA_GROUP, HEAD_DIM
    kcat = jnp.concatenate([kp_ref[...], km_ref[...], kn_ref[...]], axis=0)
    vcat = jnp.concatenate([vp_ref[...], vm_ref[...], vn_ref[...]], axis=0)
    qi = lax.broadcasted_iota(I32, (G * W, 3 * W), 0) & (W - 1)
    kj = lax.broadcasted_iota(I32, (G * W, 3 * W), 1)
    scale = d ** -0.5
    for r in range(tq // W):
        kw = kcat[r * W:(r + 3) * W]
        vw = vcat[r * W:(r + 3) * W]
        qs = jnp.concatenate([q_ref[r * W:(r + 1) * W, g * d:(g + 1) * d] for g in range(G)], axis=0)
        s = lax.dot_general(qs, kw, NT_DIMS, preferred_element_type=F32) * scale
        first = i * tq + (r - 1) * W
        lo = jnp.maximum(qi, -first)
        hi = jnp.minimum(qi + 2 * W, seq - 1 - first)
        s = jnp.where(kj >= lo, jnp.where(kj <= hi, s, -jnp.inf), -jnp.inf)
        sk = jnp.concatenate([jnp.full((W, 1), sink_ref[hk * G + g], F32) for g in range(G)], axis=0)
        mx = jnp.maximum(jnp.max(s, axis=-1, keepdims=True), sk)
        p = jnp.exp(s - mx)
        denom = jnp.sum(p, axis=-1, keepdims=True) + jnp.exp(sk - mx)
        o = jnp.dot(p.astype(BF16), vw, preferred_element_type=F32) / denom
        for g in range(G):
            o_ref[r * W:(r + 1) * W, g * d:(g + 1) * d] = o[g * W:(g + 1) * W].astype(o_ref.dtype)


def _swa_attention(proj, sink, batch, seq, n_kv, q_col, k_col, v_col, tq=512):
    T = proj.shape[0]
    W, G, d = WINDOW, SWA_GROUP, HEAD_DIM
    nq = seq // tq
    wpt = tq // W
    nw = seq // W
    qw = G * d
    kc, vc = k_col // d, v_col // d

    def prev_map(col):
        return lambda b, h, i, *_: (b * nw + jnp.maximum(i * wpt - 1, 0), col + h)

    def next_map(col):
        return lambda b, h, i, *_: (b * nw + jnp.minimum((i + 1) * wpt, nw - 1), col + h)

    def main_map(col):
        return lambda b, h, i, *_: (b * nq + i, col + h)

    grid_spec = pltpu.PrefetchScalarGridSpec(
        num_scalar_prefetch=1,
        grid=(batch, n_kv, nq),
        in_specs=[
            pl.BlockSpec((tq, qw), lambda b, h, i, *_: (b * nq + i, q_col // qw + h)),
            pl.BlockSpec((W, d), prev_map(kc)),
            pl.BlockSpec((tq, d), main_map(kc)),
            pl.BlockSpec((W, d), next_map(kc)),
            pl.BlockSpec((W, d), prev_map(vc)),
            pl.BlockSpec((tq, d), main_map(vc)),
            pl.BlockSpec((W, d), next_map(vc)),
        ],
        out_specs=pl.BlockSpec((tq, qw), lambda b, h, i, *_: (b * nq + i, h)),
    )
    return pl.pallas_call(
        functools.partial(_swa_kernel, tq=tq, seq=seq),
        grid_spec=grid_spec,
        out_shape=jax.ShapeDtypeStruct((T, n_kv * qw), BF16),
        compiler_params=_params("arbitrary", "arbitrary", "arbitrary"),
        name="swa_attention",
    )(sink.astype(F32), proj, proj, proj, proj, proj, proj, proj)


def _log_sigmoid(x):
    return jnp.minimum(x, 0.0) - jnp.log(1.0 + jnp.exp(-jnp.abs(x)))


def _mlstm_kernel(gb_ref, q_ref, k_ref, v_ref, og_ref, gc_ref, gr_ref, cwq_ref, cwk_ref, cbq_ref, cbk_ref,
                  ng_ref, o_ref, qs, ks, hf, *, seq, n_heads):
    head = pl.program_id(1)
    L, d, halo = MLSTM_CHUNK, HEAD_DIM, BF16_SUBLANES
    nc = seq // L
    pad = (CONV_WIDTH - 1) // 2

    def conv_body(c, carry):
        base = pl.multiple_of(c * L, L)
        lo = pl.multiple_of(jnp.maximum(base - halo, 0), halo)
        hi = pl.multiple_of(jnp.minimum(base + L, seq - halo), halo)
        rows = lax.broadcasted_iota(I32, (L + 2 * halo, 1), 0) + (base - halo)

        def conv(src, w_ref, b_ref):
            x = jnp.concatenate([src[pl.ds(lo, halo), :], src[pl.ds(base, L), :], src[pl.ds(hi, halo), :]],
                                axis=0).astype(F32)
            x = jnp.where(rows >= 0, jnp.where(rows < seq, x, 0.0), 0.0)
            acc = jnp.zeros((L, d), F32) + b_ref[...]
            for j in range(CONV_WIDTH):
                start = halo + j - pad
                acc = acc + x[start:start + L] * w_ref[j:j + 1, :]
            return _silu(acc)

        qs[pl.ds(base, L), :] = conv(q_ref, cwq_ref, cbq_ref).astype(BF16)
        ks[pl.ds(base, L), :] = (conv(k_ref, cwk_ref, cbk_ref) * (d ** -0.5)).astype(BF16)
        return carry

    lax.fori_loop(0, nc, conv_body, 0)

    tt = lax.broadcasted_iota(I32, (L, L), 0)
    ss = lax.broadcasted_iota(I32, (L, L), 1)

    def scan_chunk(c, carry, rev):
        C, n, m = carry
        base = pl.multiple_of(c * L, L)
        q = qs[pl.ds(base, L), :]
        k = ks[pl.ds(base, L), :]
        v = v_ref[pl.ds(base, L), :]
        gi, gf = (2, 3) if rev else (0, 1)
        bias_i = gb_ref[gi * n_heads + head]
        bias_f = gb_ref[gf * n_heads + head]
        gcol = gc_ref[pl.ds(base, L), :]
        grow = gr_ref[:, pl.ds(base, L)]
        i_col = gcol[:, gi:gi + 1] + bias_i
        f_col = _log_sigmoid(gcol[:, gf:gf + 1] + bias_f)
        i_row = grow[gi:gi + 1, :] + bias_i
        f_row = _log_sigmoid(grow[gf:gf + 1, :] + bias_f)
        incl = (ss >= tt) if rev else (ss <= tt)
        incl_t = (tt >= ss) if rev else (tt <= ss)
        b_t = jnp.dot(incl.astype(F32), jnp.broadcast_to(f_col, (L, L)), preferred_element_type=F32,
                      precision=lax.Precision.HIGHEST)
        b_s = jnp.dot(jnp.broadcast_to(f_row, (L, L)), incl_t.astype(F32), preferred_element_type=F32,
                      precision=lax.Precision.HIGHEST)
        b_col = b_t[:, 0:1]
        b_last = b_s[0:1, 0:1] if rev else b_s[0:1, L - 1:L]
        log_w = jnp.where(incl, b_t - b_s + i_row, -jnp.inf)
        log_inter = b_col + m
        m_t = jnp.maximum(log_inter, jnp.max(log_w, axis=-1, keepdims=True))
        w_intra = jnp.exp(log_w - m_t)
        w_inter = jnp.exp(log_inter - m_t)
        s = lax.dot_general(q, k, NT_DIMS, preferred_element_type=F32) * w_intra
        num = (w_inter * jnp.dot(q, C.astype(BF16), preferred_element_type=F32)
               + jnp.dot(s.astype(BF16), v, preferred_element_type=F32))
        den = (w_inter * jnp.sum(q.astype(F32) * n, axis=-1, keepdims=True)
               + jnp.sum(s, axis=-1, keepdims=True))
        h_out = num / jnp.maximum(jnp.abs(den), jnp.exp(-m_t))
        log_ws = b_last - b_col + i_col
        m_new = jnp.maximum(b_last + m, jnp.max(log_ws, axis=0, keepdims=True))
        decay = jnp.exp(b_last + m - m_new)
        kw = k.astype(F32) * jnp.exp(log_ws - m_new)
        C = decay * C + lax.dot_general(kw.astype(BF16), v, TN_DIMS, preferred_element_type=F32)
        n = decay * n + jnp.sum(kw, axis=0, keepdims=True)
        return (C, n, m_new), h_out

    init = (jnp.zeros((d, d), F32), jnp.zeros((1, d), F32), jnp.zeros((1, 1), F32))

    def fwd_body(c, carry):
        carry, h_out = scan_chunk(c, carry, False)
        hf[pl.ds(pl.multiple_of(c * L, L), L), :] = h_out
        return carry

    lax.fori_loop(0, nc, fwd_body, init)

    def bwd_body(t, carry):
        c = nc - 1 - t
        base = pl.multiple_of(c * L, L)
        carry, h_out = scan_chunk(c, carry, True)
        h = hf[pl.ds(base, L), :] + h_out
        h = h * lax.rsqrt(jnp.mean(h * h, axis=-1, keepdims=True) + EPS) * ng_ref[...]
        h = h * jax.nn.sigmoid(og_ref[pl.ds(base, L), :].astype(F32))
        o_ref[pl.ds(base, L), :] = h.astype(o_ref.dtype)
        return carry

    lax.fori_loop(0, nc, bwd_body, init)


def _mlstm(proj, gates_col, gates_row, gate_b, conv_w, conv_b, norm_g, batch, seq, n_heads, q_col):
    T = proj.shape[0]
    d = HEAD_DIM
    width = n_heads * d
    c0 = q_col // d
    col = lambda off: (lambda b, h, *_: (b, c0 + off * n_heads + h))
    seq_blk = lambda off: pl.BlockSpec((seq, d), col(off))
    grid_spec = pltpu.PrefetchScalarGridSpec(
        num_scalar_prefetch=1,
        grid=(batch, n_heads),
        in_specs=[
            seq_blk(0), seq_blk(1), seq_blk(2), seq_blk(3),
            pl.BlockSpec((None, None, seq, 4), lambda b, h, *_: (b, h, 0, 0)),
            pl.BlockSpec((None, None, 4, seq), lambda b, h, *_: (b, h, 0, 0)),
            pl.BlockSpec((CONV_WIDTH, d), lambda b, h, *_: (0, h)),
            pl.BlockSpec((CONV_WIDTH, d), lambda b, h, *_: (0, n_heads + h)),
            pl.BlockSpec((1, d), lambda b, h, *_: (0, h)),
            pl.BlockSpec((1, d), lambda b, h, *_: (0, n_heads + h)),
            pl.BlockSpec((1, d), lambda b, h, *_: (0, h)),
        ],
        out_specs=pl.BlockSpec((seq, d), lambda b, h, *_: (b, h)),
        scratch_shapes=[pltpu.VMEM((seq, d), BF16), pltpu.VMEM((seq, d), BF16), pltpu.VMEM((seq, d), F32)],
    )
    return pl.pallas_call(
        functools.partial(_mlstm_kernel, seq=seq, n_heads=n_heads),
        grid_spec=grid_spec,
        out_shape=jax.ShapeDtypeStruct((T, width), BF16),
        compiler_params=_params("arbitrary", "arbitrary"),
        name="mlstm_bidir",
    )(gate_b.astype(F32), proj, proj, proj, proj, gates_col, gates_row, conv_w, conv_w,
      conv_b.reshape(1, 2 * width), conv_b.reshape(1, 2 * width), norm_g.reshape(1, width))


def _diff_kernel(lam_ref, ng_ref, q_ref, k_ref, v_ref, o_ref, m_sc, l_sc, acc_sc, *, tq, tk, seq, lam_init):
    d = HEAD_DIM
    q = q_ref[...]
    lane = lax.broadcasted_iota(I32, q.shape, 1)
    zero = jnp.zeros_like(q)
    q2 = jnp.concatenate([jnp.where(lane < d // 2, q, zero), jnp.where(lane >= d // 2, q, zero)], axis=0)
    m_sc[...] = jnp.full(m_sc.shape, -jnp.inf, F32)
    l_sc[...] = jnp.zeros(l_sc.shape, F32)
    acc_sc[...] = jnp.zeros(acc_sc.shape, F32)

    def body(j, carry):
        off = pl.multiple_of(j * tk, tk)
        kb = k_ref[pl.ds(off, tk), :]
        vb = v_ref[pl.ds(off, tk), :]
        s = lax.dot_general(q2, kb, NT_DIMS, preferred_element_type=F32)
        m_old = m_sc[...]
        m_new = jnp.maximum(m_old, jnp.max(s, axis=-1, keepdims=True))
        alpha = jnp.exp(m_old - m_new)
        p = jnp.exp(s - m_new)
        l_sc[...] = alpha * l_sc[...] + jnp.sum(p, axis=-1, keepdims=True)
        acc_sc[...] = alpha * acc_sc[...] + jnp.dot(p.astype(BF16), vb, preferred_element_type=F32)
        m_sc[...] = m_new
        return carry

    lax.fori_loop(0, seq // tk, body, 0)

    lam = lam_ref[...]
    lam_full = (jnp.exp(jnp.sum(lam[0:1] * lam[1:2], axis=-1, keepdims=True))
                - jnp.exp(jnp.sum(lam[2:3] * lam[3:4], axis=-1, keepdims=True)) + lam_init)
    o = acc_sc[0:tq] / l_sc[0:tq] - lam_full * (acc_sc[tq:2 * tq] / l_sc[tq:2 * tq])
    o = o * lax.rsqrt(jnp.mean(o * o, axis=-1, keepdims=True) + EPS) * ng_ref[...] * (1.0 - lam_init)
    o_ref[...] = o.astype(o_ref.dtype)


def _diff_attention(proj, lam, norm_g, batch, seq, n_heads, q_col, layer_idx, tq=256, tk=512):
    T = proj.shape[0]
    d = HEAD_DIM
    nq = seq // tq
    c0 = q_col // d
    lam_init = 0.8 - 0.6 * math.exp(-0.3 * layer_idx)
    return pl.pallas_call(
        functools.partial(_diff_kernel, tq=tq, tk=tk, seq=seq, lam_init=lam_init),
        grid=(batch, n_heads, nq),
        in_specs=[
            pl.BlockSpec(lam.shape, lambda b, h, i: (0, 0)),
            pl.BlockSpec((1, d), lambda b, h, i: (0, 0)),
            pl.BlockSpec((tq, d), lambda b, h, i: (b * nq + i, c0 + h)),
            pl.BlockSpec((seq, d), lambda b, h, i: (b, c0 + n_heads + h)),
            pl.BlockSpec((seq, d), lambda b, h, i: (b, c0 + 2 * n_heads + h)),
        ],
        out_specs=pl.BlockSpec((tq, d), lambda b, h, i: (b * nq + i, h)),
        out_shape=jax.ShapeDtypeStruct((T, n_heads * d), BF16),
        scratch_shapes=[pltpu.VMEM((2 * tq, 1), F32), pltpu.VMEM((2 * tq, 1), F32),
                        pltpu.VMEM((2 * tq, d), F32)],
        compiler_params=_params("arbitrary", "arbitrary", "arbitrary"),
        name="diff_attention",
    )(lam.astype(F32), norm_g.reshape(1, d).astype(F32), proj, proj, proj)


def _outproj_kernel(a_ref, b_ref, c_ref, w_ref, x_ref, g_ref, o_ref):
    ka, kb = a_ref.shape[1], b_ref.shape[1]
    acc = jnp.dot(a_ref[...], w_ref[0:ka, :], preferred_element_type=F32)
    acc = acc + jnp.dot(b_ref[...], w_ref[ka:ka + kb, :], preferred_element_type=F32)
    acc = acc + jnp.dot(c_ref[...], w_ref[ka + kb:, :], preferred_element_type=F32)
    o_ref[...] = x_ref[...] + g_ref[0] * acc


def _out_projection(a, b, c, w, x, gate, seq, tm=512, tn=512):
    T, D = x.shape
    B = gate.shape[0]
    per_b = seq // tm
    K = w.shape[0]
    return pl.pallas_call(
        _outproj_kernel,
        grid=(T // tm, D // tn),
        in_specs=[
            pl.BlockSpec((tm, a.shape[1]), lambda i, j: (i, 0)),
            pl.BlockSpec((tm, b.shape[1]), lambda i, j: (i, 0)),
            pl.BlockSpec((tm, c.shape[1]), lambda i, j: (i, 0)),
            pl.BlockSpec((K, tn), lambda i, j: (0, j)),
            pl.BlockSpec((tm, tn), lambda i, j: (i, j)),
            pl.BlockSpec((1, 1, tn), lambda i, j: (i // per_b, 0, j)),
        ],
        out_specs=pl.BlockSpec((tm, tn), lambda i, j: (i, j)),
        out_shape=jax.ShapeDtypeStruct((T, D), F32),
        compiler_params=_params("arbitrary", "arbitrary"),
        name="out_projection_residual",
    )(a, b, c, w, x, gate.reshape(B, 1, D))


def _shared_gu_kernel(h_ref, sg_ref, su_ref, o_ref):
    h = h_ref[...]
    g = jnp.dot(h, sg_ref[...], preferred_element_type=F32)
    u = jnp.dot(h, su_ref[...], preferred_element_type=F32)
    o_ref[...] = (_silu(g) * u).astype(o_ref.dtype)


def _shared_gate_up(h, sg, su, tm=512):
    T, D = h.shape
    Fs = sg.shape[1]
    return pl.pallas_call(
        _shared_gu_kernel,
        grid=(T // tm,),
        in_specs=[pl.BlockSpec((tm, D), lambda i: (i, 0)),
                  pl.BlockSpec((D, Fs), lambda i: (0, 0)),
                  pl.BlockSpec((D, Fs), lambda i: (0, 0))],
        out_specs=pl.BlockSpec((tm, Fs), lambda i: (i, 0)),
        out_shape=jax.ShapeDtypeStruct((T, Fs), BF16),
        compiler_params=_params("arbitrary"),
        name="shared_gate_up",
    )(h, sg, su)


def _row_gather_start(idx_ref, n_rows, src_hbm, dst, sem):
    def body(r, carry):
        t = idx_ref[0, 0, r]
        pltpu.make_async_copy(src_hbm.at[pl.ds(t, 1), :], dst.at[pl.ds(r, 1), :], sem).start()
        return carry
    lax.fori_loop(0, n_rows, body, 0, unroll=8)


def _expert_kernel(be_ref, nact_ref, idx0_ref, idxn_ref, h_hbm, wg_ref, wu_ref, wd_ref, y_ref, xbuf, sem, *, tm):
    i = pl.program_id(0)
    nact = nact_ref[0]
    slot = i % 2

    @pl.when(i == 0)
    def _():
        _row_gather_start(idx0_ref, tm, h_hbm, xbuf.at[0], sem.at[0])

    @pl.when(i + 1 < nact)
    def _():
        _row_gather_start(idxn_ref, tm, h_hbm, xbuf.at[1 - slot], sem.at[1 - slot])

    @pl.when(i < nact)
    def _():
        pltpu.make_async_copy(h_hbm.at[pl.ds(0, tm), :], xbuf.at[slot], sem.at[slot]).wait()
        x = xbuf[slot].astype(BF16)
        g = jnp.dot(x, wg_ref[0], preferred_element_type=F32)
        u = jnp.dot(x, wu_ref[0], preferred_element_type=F32)
        hid = (_silu(g) * u).astype(BF16)
        y_ref[...] = jnp.dot(hid, wd_ref[0], preferred_element_type=F32)

    @pl.when(i >= nact)
    def _():
        y_ref[...] = jnp.zeros(y_ref.shape, y_ref.dtype)


def _routed_experts(h32, row_t, blk_e, nact, wg, wu, wd, tm):
    T, D = h32.shape
    E, _, Fe = wg.shape
    nblk = row_t.shape[0] // tm
    idx = row_t.reshape(nblk, 1, tm)
    grid_spec = pltpu.PrefetchScalarGridSpec(
        num_scalar_prefetch=2,
        grid=(nblk,),
        in_specs=[
            pl.BlockSpec((1, 1, tm), lambda i, be, na: (0, 0, 0), memory_space=pltpu.SMEM),
            pl.BlockSpec((1, 1, tm), lambda i, be, na: (jnp.minimum(i + 1, nblk - 1), 0, 0),
                         memory_space=pltpu.SMEM),
            pl.BlockSpec(memory_space=pl.ANY),
            pl.BlockSpec((1, D, Fe), lambda i, be, na: (be[i], 0, 0)),
            pl.BlockSpec((1, D, Fe), lambda i, be, na: (be[i], 0, 0)),
            pl.BlockSpec((1, Fe, D), lambda i, be, na: (be[i], 0, 0)),
        ],
        out_specs=pl.BlockSpec((tm, D), lambda i, be, na: (i, 0)),
        scratch_shapes=[pltpu.VMEM((2, tm, D), F32), pltpu.SemaphoreType.DMA((2,))],
    )
    return pl.pallas_call(
        functools.partial(_expert_kernel, tm=tm),
        grid_spec=grid_spec,
        out_shape=jax.ShapeDtypeStruct((nblk * tm, D), F32),
        compiler_params=_params("arbitrary"),
        name="routed_experts",
    )(blk_e, nact, idx, idx, h32, wg, wu, wd)


def _combine_kernel(pos0_ref, posn_ref, w_ref, hs_ref, sd_ref, x_ref, g_ref, y_hbm, o_ref, buf, sem, *, tc, top_k):
    i = pl.program_id(0)
    n = pl.num_programs(0)
    slot = i % 2

    def start(pos_ref, s):
        def body(r, carry):
            p = pos_ref[0, 0, r]
            k = r // tc
            pltpu.make_async_copy(y_hbm.at[pl.ds(p, 1), :], buf.at[s, k, pl.ds(r - k * tc, 1), :],
                                  sem.at[s]).start()
            return carry
        lax.fori_loop(0, top_k * tc, body, 0, unroll=8)

    @pl.when(i == 0)
    def _():
        start(pos0_ref, 0)

    @pl.when(i + 1 < n)
    def _():
        start(posn_ref, 1 - slot)

    for k in range(top_k):
        pltpu.make_async_copy(y_hbm.at[pl.ds(0, tc), :], buf.at[slot, k], sem.at[slot]).wait()
    w = w_ref[...]
    routed = buf[slot, 0] * w[:, 0:1]
    for k in range(1, top_k):
        routed = routed + buf[slot, k] * w[:, k:k + 1]
    shared = jnp.dot(hs_ref[...], sd_ref[...], preferred_element_type=F32)
    o_ref[...] = x_ref[...] + g_ref[0] * (routed + shared)


def _moe_combine(y, pos, w, hs, sd, x, gate, seq, tc=128):
    T, D = x.shape
    B = gate.shape[0]
    K = pos.shape[1]
    Fs = hs.shape[1]
    nt = T // tc
    per_b = seq // tc
    pos_tiles = pos.reshape(nt, tc, K).transpose(0, 2, 1).reshape(nt, 1, K * tc)
    return pl.pallas_call(
        functools.partial(_combine_kernel, tc=tc, top_k=K),
        grid=(nt,),
        in_specs=[
            pl.BlockSpec((1, 1, K * tc), lambda i: (0, 0, 0), memory_space=pltpu.SMEM),
            pl.BlockSpec((1, 1, K * tc), lambda i: (jnp.minimum(i + 1, nt - 1), 0, 0), memory_space=pltpu.SMEM),
            pl.BlockSpec((tc, K), lambda i: (i, 0)),
            pl.BlockSpec((tc, Fs), lambda i: (i, 0)),
            pl.BlockSpec((Fs, D), lambda i: (0, 0)),
            pl.BlockSpec((tc, D), lambda i: (i, 0)),
            pl.BlockSpec((1, 1, D), lambda i: (i // per_b, 0, 0)),
            pl.BlockSpec(memory_space=pl.ANY),
        ],
        out_specs=pl.BlockSpec((tc, D), lambda i: (i, 0)),
        out_shape=jax.ShapeDtypeStruct((T, D), F32),
        scratch_shapes=[pltpu.VMEM((2, K, tc, D), F32), pltpu.SemaphoreType.DMA((2,))],
        compiler_params=_params("arbitrary"),
        name="moe_combine_residual",
    )(pos_tiles, pos_tiles, w, hs, sd, x, gate.reshape(B, 1, D), y)


def _route(logits, router_b):
    T = logits.shape[0]
    scores = jax.nn.sigmoid(logits)
    biased = scores + router_b.astype(F32)
    per_group = N_EXPERTS // N_GROUPS
    gscore = lax.top_k(biased.reshape(T, N_GROUPS, per_group), 2)[0].sum(-1)
    _, gidx = lax.top_k(gscore, TOPK_GROUPS)
    gmask = (gidx[:, :, None] == jnp.arange(N_GROUPS)[None, None, :]).any(1)
    emask = jnp.repeat(gmask, per_group, axis=1)
    _, eidx = lax.top_k(jnp.where(emask, biased, -jnp.inf), TOP_K)
    w = jnp.take_along_axis(scores, eidx, axis=1)
    w = w / w.sum(-1, keepdims=True) * ROUTED_SCALE
    return eidx, w


def _dispatch_plan(eidx, tm):
    T, K = eidx.shape
    A = T * K
    E = N_EXPERTS
    flat_e = eidx.reshape(A).astype(I32)
    order = jnp.argsort(flat_e).astype(I32)
    se = flat_e[order]
    st = order // K
    counts = jnp.zeros((E,), I32).at[flat_e].add(1)
    padded = (counts + tm - 1) // tm * tm
    pad_end = jnp.cumsum(padded)
    pad_start = pad_end - padded
    cnt_start = jnp.cumsum(counts) - counts
    dest = pad_start[se] + jnp.arange(A, dtype=I32) - cnt_start[se]
    nblk = A // tm + E
    row_t = jnp.zeros((nblk * tm,), I32).at[dest].set(st)
    pos = jnp.zeros((A,), I32).at[order].set(dest).reshape(T, K)
    nact = (pad_end[-1] // tm).astype(I32)
    blk = jnp.arange(nblk, dtype=I32)
    blk_e = jnp.clip(jnp.searchsorted(pad_end, blk * tm, side='right'), 0, E - 1).astype(I32)
    blk_e = blk_e[jnp.minimum(blk, nact - 1)]
    return row_t, blk_e, nact.reshape(1), pos


def kernel(x, c, ada_w, ada_b, ada_table, norm1_g, w_in, swa_sink, mlstm_conv_w, mlstm_conv_b, mlstm_gate_b,
           mlstm_norm_g, diff_lambda, diff_norm_g, w_out, norm2_g, router_w, router_b, exp_gate, exp_up,
           exp_down, sh_gate, sh_up, sh_down, final_g):
    B, S, D = x.shape
    depth = w_in.shape[0]
    T = B * S
    d = HEAD_DIM
    n_swa_q = swa_sink.shape[1]
    n_swa_kv = n_swa_q // SWA_GROUP
    n_ml = mlstm_gate_b.shape[1] // 4
    n_diff = (D // d) - n_swa_q - n_ml
    swa_q_w, swa_kv_w, ml_w, diff_w = n_swa_q * d, n_swa_kv * d, n_ml * d, n_diff * d
    gate_w = 4 * n_ml
    gate_off = swa_q_w + 2 * swa_kv_w + 4 * ml_w
    k_a, v_a = swa_q_w, swa_q_w + swa_kv_w
    q_m = swa_q_w + 2 * swa_kv_w
    q_c = gate_off
    tn = 512
    diff_q_scale = (d // 2) ** -0.5
    assert math.frexp(diff_q_scale)[0] == 0.5
    tiles = ((swa_q_w + swa_kv_w) // tn, q_c // tn, (q_c + diff_w) // tn, (q_c + 2 * diff_w) // tn, diff_q_scale)

    cos_a, sin_a = _rope_tables(S, d // 2, tn)
    cos_c, sin_c = _rope_tables(S, d // 4, tn)
    rope_tabs = (cos_a, sin_a, cos_c, sin_c)

    mod = _modulation(c, ada_w, ada_b, ada_table)
    xt = x.reshape(T, D)
    for l in range(depth):
        sh_a, sc_a, g_a, sh_f, sc_f, g_f = [mod[l, :, i] for i in range(ADA_CHUNKS)]
        w_main = jnp.concatenate([w_in[l, :, :gate_off], w_in[l, :, gate_off + gate_w:]], axis=1).astype(BF16)
        w_gate = jnp.pad(w_in[l, :, gate_off:gate_off + gate_w], ((0, 0), (0, LANES - gate_w)))
        h, gates = _norm_mod(xt, norm1_g[l], sc_a, sh_a, w_gate, S, emit_f32=False)
        proj = _in_projection(h, w_main, rope_tabs, S, tiles, tn=tn)
        out_a = _swa_attention(proj, swa_sink[l], B, S, n_swa_kv, 0, k_a, v_a)
        g4 = gates[:, :gate_w].reshape(B, S, 4, n_ml)
        out_b = _mlstm(proj, g4.transpose(0, 3, 1, 2), g4.transpose(0, 3, 2, 1), mlstm_gate_b[l],
                       mlstm_conv_w[l], mlstm_conv_b[l], mlstm_norm_g[l], B, S, n_ml, q_m)
        out_c = _diff_attention(proj, diff_lambda[l], diff_norm_g[l], B, S, n_diff, q_c, l)
        xt = _out_projection(out_a, out_b, out_c, w_out[l].astype(BF16), xt, g_a, S)

        w_router = jnp.pad(router_w[l], ((0, 0), (0, LANES - N_EXPERTS)))
        h, h32, logits = _norm_mod(xt, norm2_g[l], sc_f, sh_f, w_router, S, emit_f32=True)
        eidx, wts = _route(logits[:, :N_EXPERTS], router_b[l])
        tm = 256
        row_t, blk_e, nact, pos = _dispatch_plan(eidx, tm)
        y = _routed_experts(h32, row_t, blk_e, nact, exp_gate[l].astype(BF16), exp_up[l].astype(BF16),
                            exp_down[l].astype(BF16), tm)
        hs = _shared_gate_up(h, sh_gate[l].astype(BF16), sh_up[l].astype(BF16))
        xt = _moe_combine(y, pos, wts, hs, sh_down[l].astype(BF16), xt, g_f, S)
    return _final_norm(xt, final_g).reshape(B, S, D)
```

```python
import functools
import math

import jax
import jax.numpy as jnp
from jax import lax
from jax.experimental import pallas as pl
from jax.experimental.pallas import tpu as pltpu

F32 = jnp.float32
BF16 = jnp.bfloat16
I32 = jnp.int32

HEAD_DIM = 128
SWA_GROUP = 3
WINDOW = 128
MLSTM_CHUNK = 128
CONV_WIDTH = 5
ROPE_THETA = 10000.0
N_EXPERTS = 48
TOP_K = 6
N_GROUPS = 8
TOPK_GROUPS = 4
ROUTED_SCALE = 2.5
ADA_CHUNKS = 6
EPS = 1e-6

LANES = 128
BF16_SUBLANES = 16
MXU_WIDTH_V7X = 256
DIFF_QUERY_STRIP = MXU_WIDTH_V7X // 2
VMEM_LIMIT_V7X = 56 * 1024 * 1024

NT_DIMS = (((1,), (1,)), ((), ()))
TN_DIMS = (((0,), (0,)), ((), ()))


def _params(*sem):
    return pltpu.CompilerParams(dimension_semantics=sem, vmem_limit_bytes=VMEM_LIMIT_V7X)


def _silu(x):
    return x * jax.nn.sigmoid(x)


def _mod_kernel(c_ref, w_ref, b_ref, tab_ref, o_ref):
    a = _silu(c_ref[...])
    cond = jnp.dot(a.astype(BF16), w_ref[...].astype(BF16), preferred_element_type=F32) + b_ref[...]
    for l in range(tab_ref.shape[0]):
        o_ref[l] = cond + tab_ref[l]


def _modulation(c, ada_w, ada_b, ada_table, tn=512):
    B, D = c.shape
    depth = ada_table.shape[0]
    N = ada_w.shape[1]
    rows = 8
    c_pad = jnp.pad(c, ((0, rows - B), (0, 0)))
    out = pl.pallas_call(
        _mod_kernel,
        grid=(N // tn,),
        in_specs=[
            pl.BlockSpec((rows, D), lambda j: (0, 0)),
            pl.BlockSpec((D, tn), lambda j: (0, j)),
            pl.BlockSpec((1, tn), lambda j: (0, j)),
            pl.BlockSpec((depth, 1, tn), lambda j: (0, 0, j)),
        ],
        out_specs=pl.BlockSpec((depth, rows, tn), lambda j: (0, 0, j)),
        out_shape=jax.ShapeDtypeStruct((depth, rows, N), F32),
        compiler_params=_params("arbitrary"),
        name="adaln_mod",
    )(c_pad, ada_w, ada_b.reshape(1, N), ada_table.reshape(depth, 1, N))
    return out[:, :B].reshape(depth, B, ADA_CHUNKS, D)


def _norm_kernel(x_ref, g_ref, sc_ref, sh_ref, ws_ref, *out_refs, emit_f32):
    x = x_ref[...]
    ms = jnp.mean(x * x, axis=-1, keepdims=True)
    h = x * lax.rsqrt(ms + EPS) * g_ref[...]
    h = h * (1.0 + sc_ref[0]) + sh_ref[0]
    out_refs[0][...] = h.astype(BF16)
    if emit_f32:
        out_refs[1][...] = h
    out_refs[-1][...] = jnp.dot(h, ws_ref[...], preferred_element_type=F32,
                                precision=lax.Precision.HIGHEST)


def _norm_mod(x, g, scale, shift, w_small, seq, emit_f32, tr=256):
    T, D = x.shape
    B = scale.shape[0]
    ns = w_small.shape[1]
    per_b = seq // tr
    row = lambda i: (i, 0)
    out_shape = [jax.ShapeDtypeStruct((T, D), BF16)]
    out_specs = [pl.BlockSpec((tr, D), row)]
    if emit_f32:
        out_shape.append(jax.ShapeDtypeStruct((T, D), F32))
        out_specs.append(pl.BlockSpec((tr, D), row))
    out_shape.append(jax.ShapeDtypeStruct((T, ns), F32))
    out_specs.append(pl.BlockSpec((tr, ns), row))
    return pl.pallas_call(
        functools.partial(_norm_kernel, emit_f32=emit_f32),
        grid=(T // tr,),
        in_specs=[
            pl.BlockSpec((tr, D), row),
            pl.BlockSpec((1, D), lambda i: (0, 0)),
            pl.BlockSpec((1, 1, D), lambda i: (i // per_b, 0, 0)),
            pl.BlockSpec((1, 1, D), lambda i: (i // per_b, 0, 0)),
            pl.BlockSpec((D, ns), lambda i: (0, 0)),
        ],
        out_specs=out_specs,
        out_shape=out_shape,
        compiler_params=_params("arbitrary"),
        name="rmsnorm_adaln",
    )(x, g.reshape(1, D), scale.reshape(B, 1, D), shift.reshape(B, 1, D), w_small)


def _final_norm_kernel(x_ref, g_ref, o_ref):
    x = x_ref[...]
    ms = jnp.mean(x * x, axis=-1, keepdims=True)
    o_ref[...] = x * lax.rsqrt(ms + EPS) * g_ref[...]


def _final_norm(x, g, tr=256):
    T, D = x.shape
    return pl.pallas_call(
        _final_norm_kernel,
        grid=(T // tr,),
        in_specs=[pl.BlockSpec((tr, D), lambda i: (i, 0)), pl.BlockSpec((1, D), lambda i: (0, 0))],
        out_specs=pl.BlockSpec((tr, D), lambda i: (i, 0)),
        out_shape=jax.ShapeDtypeStruct((T, D), F32),
        compiler_params=_params("arbitrary"),
        name="final_rmsnorm",
    )(x, g.reshape(1, D))


def _rope_tile(acc, cos, sin, half):
    n = acc.shape[1]
    lane = lax.broadcasted_iota(I32, acc.shape, 1)
    from_right = pltpu.roll(acc, n - half, 1)
    from_left = pltpu.roll(acc, half, 1)
    rot = jnp.where((lane & (2 * half - 1)) < half, from_right, from_left)
    return acc * cos + rot * sin


def _inproj_kernel(a_ref, w_ref, cos_a, sin_a, cos_c, sin_c, o_ref, *, tiles):
    n_aqk, c_q0, c_k0, c_v0, q_scale = tiles
    j = pl.program_id(1)
    acc = jnp.dot(a_ref[...], w_ref[...], preferred_element_type=F32)

    @pl.when(j < n_aqk)
    def _():
        o_ref[...] = _rope_tile(acc, cos_a[...], sin_a[...], HEAD_DIM // 2).astype(o_ref.dtype)

    @pl.when(jnp.logical_and(j >= c_q0, j < c_k0))
    def _():
        o_ref[...] = (_rope_tile(acc, cos_c[...], sin_c[...], HEAD_DIM // 4) * q_scale).astype(o_ref.dtype)

    @pl.when(jnp.logical_and(j >= c_k0, j < c_v0))
    def _():
        o_ref[...] = _rope_tile(acc, cos_c[...], sin_c[...], HEAD_DIM // 4).astype(o_ref.dtype)

    @pl.when(jnp.logical_or(jnp.logical_and(j >= n_aqk, j < c_q0), j >= c_v0))
    def _():
        o_ref[...] = acc.astype(o_ref.dtype)


def _in_projection(h, w, rope_tabs, seq, tiles, tm=512, tn=512):
    T, D = h.shape
    N = w.shape[1]
    per_b = seq // tm
    tab = pl.BlockSpec((tm, tn), lambda i, j: (i % per_b, 0))
    return pl.pallas_call(
        functools.partial(_inproj_kernel, tiles=tiles),
        grid=(T // tm, N // tn),
        in_specs=[
            pl.BlockSpec((tm, D), lambda i, j: (i, 0)),
            pl.BlockSpec((D, tn), lambda i, j: (0, j)),
            tab, tab, tab, tab,
        ],
        out_specs=pl.BlockSpec((tm, tn), lambda i, j: (i, j)),
        out_shape=jax.ShapeDtypeStruct((T, N), BF16),
        compiler_params=_params("arbitrary", "arbitrary"),
        name="in_projection_rope",
    )(h, w, *rope_tabs)


def _rope_tables(seq, half, width):
    inv = jnp.power(ROPE_THETA, -jnp.arange(half, dtype=F32) / half)
    ang = jnp.arange(seq, dtype=F32)[:, None] * inv[None, :]
    cos, sin = jnp.cos(ang), jnp.sin(ang)
    reps = width // (2 * half)
    return (jnp.tile(jnp.concatenate([cos, cos], axis=1), (1, reps)),
            jnp.tile(jnp.concatenate([-sin, sin], axis=1), (1, reps)))


def _swa_kernel(sink_ref, q_ref, kp_ref, km_ref, kn_ref, vp_ref, vm_ref, vn_ref, o_ref, *, tq, seq):
    hk = pl.program_id(1)
    i = pl.program_id(2)
    W, G, d = WINDOW, SWA_GROUP, HEAD_DIM
    kcat = jnp.concatenate([kp_ref[...], km_ref[...], kn_ref[...]], axis=0)
    vcat = jnp.concatenate([vp_ref[...], vm_ref[...], vn_ref[...]], axis=0)
    qi = lax.broadcasted_iota(I32, (G * W, 3 * W), 0) & (W - 1)
    kj = lax.broadcasted_iota(I32, (G * W, 3 * W), 1)
    scale = d ** -0.5
    for r in range(tq // W):
        kw = kcat[r * W:(r + 3) * W]
        vw = vcat[r * W:(r + 3) * W]
        qs = jnp.concatenate([q_ref[r * W:(r + 1) * W, g * d:(g + 1) * d] for g in range(G)], axis=0)
        s = lax.dot_general(qs, kw, NT_DIMS, preferred_element_type=F32) * scale
        first = i * tq + (r - 1) * W
        lo = jnp.maximum(qi, -first)
        hi = jnp.minimum(qi + 2 * W, seq - 1 - first)
        s = jnp.where(kj >= lo, jnp.where(kj <= hi, s, -jnp.inf), -jnp.inf)
        sk = jnp.concatenate([jnp.full((W, 1), sink_ref[hk * G + g], F32) for g in range(G)], axis=0)
        mx = jnp.maximum(jnp.max(s, axis=-1, keepdims=True), sk)
        p = jnp.exp(s - mx)
        denom = jnp.sum(p, axis=-1, keepdims=True) + jnp.exp(sk - mx)
        o = jnp.dot(p.astype(BF16), vw, preferred_element_type=F32) / denom
        for g in range(G):
            o_ref[r * W:(r + 1) * W, g * d:(g + 1) * d] = o[g * W:(g + 1) * W].astype(o_ref.dtype)


def _swa_attention(proj, sink, batch, seq, n_kv, q_col, k_col, v_col, tq=512):
    T = proj.shape[0]
    W, G, d = WINDOW, SWA_GROUP, HEAD_DIM
    nq = seq // tq
    wpt = tq // W
    nw = seq // W
    qw = G * d
    kc, vc = k_col // d, v_col // d

    def prev_map(col):
        return lambda b, h, i, *_: (b * nw + jnp.maximum(i * wpt - 1, 0), col + h)

    def next_map(col):
        return lambda b, h, i, *_: (b * nw + jnp.minimum((i + 1) * wpt, nw - 1), col + h)

    def main_map(col):
        return lambda b, h, i, *_: (b * nq + i, col + h)

    grid_spec = pltpu.PrefetchScalarGridSpec(
        num_scalar_prefetch=1,
        grid=(batch, n_kv, nq),
        in_specs=[
            pl.BlockSpec((tq, qw), lambda b, h, i, *_: (b * nq + i, q_col // qw + h)),
            pl.BlockSpec((W, d), prev_map(kc)),
            pl.BlockSpec((tq, d), main_map(kc)),
            pl.BlockSpec((W, d), next_map(kc)),
            pl.BlockSpec((W, d), prev_map(vc)),
            pl.BlockSpec((tq, d), main_map(vc)),
            pl.BlockSpec((W, d), next_map(vc)),
        ],
        out_specs=pl.BlockSpec((tq, qw), lambda b, h, i, *_: (b * nq + i, h)),
    )
    return pl.pallas_call(
        functools.partial(_swa_kernel, tq=tq, seq=seq),
        grid_spec=grid_spec,
        out_shape=jax.ShapeDtypeStruct((T, n_kv * qw), BF16),
        compiler_params=_params("arbitrary", "arbitrary", "arbitrary"),
        name="swa_attention",
    )(sink.astype(F32), proj, proj, proj, proj, proj, proj, proj)


def _log_sigmoid(x):
    return jnp.minimum(x, 0.0) - jnp.log(1.0 + jnp.exp(-jnp.abs(x)))


def _split3_bf16(x):
    hi = x.astype(BF16)
    r = x - hi.astype(F32)
    mid = r.astype(BF16)
    lo = (r - mid.astype(F32)).astype(BF16)
    return hi, mid, lo


def _mlstm_kernel(gb_ref, q_ref, k_ref, v_ref, og_ref, gr_ref, cwq_ref, cwk_ref, cbq_ref, cbk_ref, ng_ref, o_ref,
                  qs, kst, vaug, hf, hb, rmat, *, seq, n_heads, hp):
    pair = pl.program_id(1)
    L, d, halo = MLSTM_CHUNK, HEAD_DIM, BF16_SUBLANES
    nc = seq // L
    pad = (CONV_WIDTH - 1) // 2
    w = hp * d

    tt = lax.broadcasted_iota(I32, (L, L), 0)
    ss = lax.broadcasted_iota(I32, (L, L), 1)
    for rev in (False, True):
        incl_t = (tt >= ss) if rev else (tt <= ss)
        rmat[int(rev), 0:L, :] = jnp.ones((L, L), BF16)
        rmat[int(rev), L:2 * L, :] = jnp.where(incl_t, -1.0, 0.0).astype(BF16)

    def conv_body(c, carry):
        base = pl.multiple_of(c * L, L)
        lo = pl.multiple_of(jnp.maximum(base - halo, 0), halo)
        hi = pl.multiple_of(jnp.minimum(base + L, seq - halo), halo)
        rows = lax.broadcasted_iota(I32, (L + 2 * halo, 1), 0) + (base - halo)

        def conv(src, w_ref, b_ref):
            x = jnp.concatenate([src[pl.ds(lo, halo), :], src[pl.ds(base, L), :], src[pl.ds(hi, halo), :]],
                                axis=0).astype(F32)
            x = jnp.where(rows >= 0, jnp.where(rows < seq, x, 0.0), 0.0)
            acc = jnp.zeros((L, w), F32) + b_ref[...]
            for j in range(CONV_WIDTH):
                start = halo + j - pad
                acc = acc + x[start:start + L] * w_ref[j:j + 1, :]
            return _silu(acc)

        qs[pl.ds(base, L), :] = conv(q_ref, cwq_ref, cbq_ref).astype(BF16)
        kc = conv(k_ref, cwk_ref, cbk_ref) * (d ** -0.5)
        for hh in range(hp):
            kst[hh * d:(hh + 1) * d, pl.ds(base, L)] = kc[:, hh * d:(hh + 1) * d].T.astype(BF16)
            vaug[pl.ds(base, L), 2 * hh * d:(2 * hh + 1) * d] = v_ref[pl.ds(base, L), hh * d:(hh + 1) * d]
            vaug[pl.ds(base, L), (2 * hh + 1) * d:(2 * hh + 2) * d] = jnp.ones((L, d), BF16)
        return carry

    lax.fori_loop(0, nc, conv_body, 0)

    def scan_chunk(c, carry, rev, hh):
        Caug, m = carry
        base = pl.multiple_of(c * L, L)
        head = pair * hp + hh
        gi, gf = (2, 3) if rev else (0, 1)
        q = qs[pl.ds(base, L), hh * d:(hh + 1) * d]
        kt = kst[hh * d:(hh + 1) * d, pl.ds(base, L)]
        va = vaug[pl.ds(base, L), 2 * hh * d:(2 * hh + 2) * d]
        i_row = gr_ref[hh, gi:gi + 1, pl.ds(base, L)] + gb_ref[gi * n_heads + head]
        f_row = _log_sigmoid(gr_ref[hh, gf:gf + 1, pl.ds(base, L)] + gb_ref[gf * n_heads + head])
        incl = (ss >= tt) if rev else (ss <= tt)
        dmat = jnp.zeros((L, L), F32)
        for part in _split3_bf16(f_row):
            fb = jnp.broadcast_to(part.astype(F32), (L, L))
            x = jnp.concatenate([jnp.where(incl, fb, 0.0), fb], axis=1).astype(BF16)
            dmat = dmat + jnp.dot(x, rmat[int(rev)], preferred_element_type=F32)
        first, last = (L - 1, 0) if rev else (0, L - 1)
        f_first = f_row[:, first:first + 1]
        b_col = dmat[:, first:first + 1] + f_first
        b_last = dmat[last:last + 1, first:first + 1] + f_first
        log_w = jnp.where(incl, dmat + i_row, -jnp.inf)
        log_inter = b_col + m
        m_t = jnp.maximum(log_inter, jnp.max(log_w, axis=-1, keepdims=True))
        w_intra = jnp.exp(log_w - m_t)
        w_inter = jnp.exp(log_inter - m_t)
        s = jnp.dot(q, kt, preferred_element_type=F32) * w_intra
        numden = (w_inter * jnp.dot(q, Caug.astype(BF16), preferred_element_type=F32)
                  + jnp.dot(s.astype(BF16), va, preferred_element_type=F32))
        inv = 1.0 / jnp.maximum(jnp.abs(numden[:, d:d + 1]), jnp.exp(-m_t))
        h_out = numden[:, 0:d] * inv
        log_ws = dmat[last:last + 1, :] + i_row
        m_new = jnp.maximum(b_last + m, jnp.max(log_ws, axis=-1, keepdims=True))
        decay = jnp.exp(b_last + m - m_new)
        kw = (kt.astype(F32) * jnp.exp(log_ws - m_new)).astype(BF16)
        Caug = decay * Caug + jnp.dot(kw, va, preferred_element_type=F32)
        return (Caug, m_new), h_out

    def scan_body(t, carries):
        out = []
        for hh in range(hp):
            for rev in (False, True):
                c = (nc - 1 - t) if rev else t
                carry, h_out = scan_chunk(c, carries[2 * hh + int(rev)], rev, hh)
                dst = hb if rev else hf
                dst[pl.ds(pl.multiple_of(c * L, L), L), hh * d:(hh + 1) * d] = h_out
                out.append(carry)
        return tuple(out)

    init = tuple((jnp.zeros((d, 2 * d), F32), jnp.zeros((1, 1), F32)) for _ in range(2 * hp))
    lax.fori_loop(0, nc, scan_body, init)

    slab = 4 * L

    def out_body(c, carry):
        base = pl.multiple_of(c * slab, slab)
        gate = jax.nn.sigmoid(og_ref[pl.ds(base, slab), :].astype(F32))
        for hh in range(hp):
            cols = slice(hh * d, (hh + 1) * d)
            h = hf[pl.ds(base, slab), cols] + hb[pl.ds(base, slab), cols]
            h = h * lax.rsqrt(jnp.mean(h * h, axis=-1, keepdims=True) + EPS) * ng_ref[:, cols]
            o_ref[pl.ds(base, slab), cols] = (h * gate[:, cols]).astype(o_ref.dtype)
        return carry

    lax.fori_loop(0, seq // slab, out_body, 0)


def _mlstm(proj, gates_row, gate_b, conv_w, conv_b, norm_g, batch, seq, n_heads, q_col, hp=2):
    T = proj.shape[0]
    d = HEAD_DIM
    width = n_heads * d
    w = hp * d
    c0 = q_col // w
    npair = n_heads // hp
    col = lambda off: (lambda b, h, *_: (b, c0 + off * npair + h))
    seq_blk = lambda off: pl.BlockSpec((seq, w), col(off))
    grid_spec = pltpu.PrefetchScalarGridSpec(
        num_scalar_prefetch=1,
        grid=(batch, npair),
        in_specs=[
            seq_blk(0), seq_blk(1), seq_blk(2), seq_blk(3),
            pl.BlockSpec((None, hp, 4, seq), lambda b, h, *_: (b, h, 0, 0)),
            pl.BlockSpec((CONV_WIDTH, w), lambda b, h, *_: (0, h)),
            pl.BlockSpec((CONV_WIDTH, w), lambda b, h, *_: (0, npair + h)),
            pl.BlockSpec((1, w), lambda b, h, *_: (0, h)),
            pl.BlockSpec((1, w), lambda b, h, *_: (0, npair + h)),
            pl.BlockSpec((1, w), lambda b, h, *_: (0, h)),
        ],
        out_specs=pl.BlockSpec((seq, w), lambda b, h, *_: (b, h)),
        scratch_shapes=[pltpu.VMEM((seq, w), BF16), pltpu.VMEM((w, seq), BF16), pltpu.VMEM((seq, 2 * w), BF16),
                        pltpu.VMEM((seq, w), F32), pltpu.VMEM((seq, w), F32),
                        pltpu.VMEM((2, 2 * MLSTM_CHUNK, MLSTM_CHUNK), BF16)],
    )
    return pl.pallas_call(
        functools.partial(_mlstm_kernel, seq=seq, n_heads=n_heads, hp=hp),
        grid_spec=grid_spec,
        out_shape=jax.ShapeDtypeStruct((T, width), BF16),
        compiler_params=_params("arbitrary", "arbitrary"),
        name="mlstm_bidir",
    )(gate_b.astype(F32), proj, proj, proj, proj, gates_row, conv_w, conv_w,
      conv_b.reshape(1, 2 * width), conv_b.reshape(1, 2 * width), norm_g.reshape(1, width))


def _diff_kernel(lam_ref, ng_ref, q_ref, k_ref, v_ref, o_ref, vt_sc, qt_sc, acc_sc, *, tq, tk, seq, lam_init):
    d = HEAD_DIM
    qs = DIFF_QUERY_STRIP

    @pl.when(pl.program_id(2) == 0)
    def _():
        for c in range(seq // tk):
            vt_sc[0:d, c * tk:(c + 1) * tk] = v_ref[c * tk:(c + 1) * tk, :].astype(F32).T.astype(BF16)
        vt_sc[d:, :] = jnp.ones((vt_sc.shape[0] - d, seq), BF16)

    qt = q_ref[...].astype(F32).T
    row = lax.broadcasted_iota(I32, (d, qs), 0)
    for s in range(tq // qs):
        qts = qt[:, s * qs:(s + 1) * qs]
        qt_sc[:, 2 * s * qs:(2 * s + 1) * qs] = jnp.where(row < d // 2, qts, 0.0).astype(BF16)
        qt_sc[:, (2 * s + 1) * qs:(2 * s + 2) * qs] = jnp.where(row >= d // 2, qts, 0.0).astype(BF16)

    lam = lam_ref[...]
    lam_full = (jnp.exp(jnp.sum(lam[0:1] * lam[1:2], axis=-1, keepdims=True))
                - jnp.exp(jnp.sum(lam[2:3] * lam[3:4], axis=-1, keepdims=True)) + lam_init)

    n_strips = tq // qs
    nk = seq // tk
    m = jnp.full((1, 2 * tq), -jnp.inf, F32)
    acc_sc[...] = jnp.zeros(acc_sc.shape, F32)
    scores = lambda j: jnp.dot(k_ref[j * tk:(j + 1) * tk, :], qt_sc[...], preferred_element_type=F32)
    st = scores(0)
    for j in range(nk):
        st_next = scores(j + 1) if j + 1 < nk else None
        m_new = jnp.maximum(m, jnp.max(st, axis=0, keepdims=True))
        p = jnp.exp2(st - m_new).astype(BF16)
        acc_sc[...] = jnp.exp2(m - m_new) * acc_sc[...] + jnp.dot(vt_sc[:, j * tk:(j + 1) * tk], p,
                                                                preferred_element_type=F32)
        m = m_new
        st = st_next
    for s in range(n_strips):
        acc = acc_sc[:, 2 * s * qs:(2 * s + 2) * qs]
        num = acc[0:d] / acc[d:d + 1]
        ot = num[:, 0:qs] - lam_full * num[:, qs:2 * qs]
        o = ot.T
        o = o * lax.rsqrt(jnp.mean(o * o, axis=-1, keepdims=True) + EPS) * ng_ref[...] * (1.0 - lam_init)
        o_ref[s * qs:(s + 1) * qs, :] = o.astype(o_ref.dtype)


def _diff_attention(proj, lam, norm_g, batch, seq, n_heads, q_col, layer_idx, tq=1024, tk=512):
    T = proj.shape[0]
    d = HEAD_DIM
    nq = seq // tq
    c0 = q_col // d
    lam_init = 0.8 - 0.6 * math.exp(-0.3 * layer_idx)
    return pl.pallas_call(
        functools.partial(_diff_kernel, tq=tq, tk=tk, seq=seq, lam_init=lam_init),
        grid=(batch, n_heads, nq),
        in_specs=[
            pl.BlockSpec(lam.shape, lambda b, h, i: (0, 0)),
            pl.BlockSpec((1, d), lambda b, h, i: (0, 0)),
            pl.BlockSpec((tq, d), lambda b, h, i: (b * nq + i, c0 + h)),
            pl.BlockSpec((seq, d), lambda b, h, i: (b, c0 + n_heads + h)),
            pl.BlockSpec((seq, d), lambda b, h, i: (b, c0 + 2 * n_heads + h)),
        ],
        out_specs=pl.BlockSpec((tq, d), lambda b, h, i: (b * nq + i, h)),
        out_shape=jax.ShapeDtypeStruct((T, n_heads * d), BF16),
        scratch_shapes=[pltpu.VMEM((d + BF16_SUBLANES, seq), BF16), pltpu.VMEM((d, 2 * tq), BF16),
                        pltpu.VMEM((d + BF16_SUBLANES, 2 * tq), F32)],
        compiler_params=_params("arbitrary", "arbitrary", "arbitrary"),
        name="diff_attention",
    )(lam.astype(F32), norm_g.reshape(1, d).astype(F32), proj, proj, proj)


def _outproj_kernel(a_ref, b_ref, c_ref, w_ref, x_ref, g_ref, o_ref):
    ka, kb = a_ref.shape[1], b_ref.shape[1]
    acc = jnp.dot(a_ref[...], w_ref[0:ka, :], preferred_element_type=F32)
    acc = acc + jnp.dot(b_ref[...], w_ref[ka:ka + kb, :], preferred_element_type=F32)
    acc = acc + jnp.dot(c_ref[...], w_ref[ka + kb:, :], preferred_element_type=F32)
    o_ref[...] = x_ref[...] + g_ref[0] * acc


def _out_projection(a, b, c, w, x, gate, seq, tm=512, tn=512):
    T, D = x.shape
    B = gate.shape[0]
    per_b = seq // tm
    K = w.shape[0]
    return pl.pallas_call(
        _outproj_kernel,
        grid=(T // tm, D // tn),
        in_specs=[
            pl.BlockSpec((tm, a.shape[1]), lambda i, j: (i, 0)),
            pl.BlockSpec((tm, b.shape[1]), lambda i, j: (i, 0)),
            pl.BlockSpec((tm, c.shape[1]), lambda i, j: (i, 0)),
            pl.BlockSpec((K, tn), lambda i, j: (0, j)),
            pl.BlockSpec((tm, tn), lambda i, j: (i, j)),
            pl.BlockSpec((1, 1, tn), lambda i, j: (i // per_b, 0, j)),
        ],
        out_specs=pl.BlockSpec((tm, tn), lambda i, j: (i, j)),
        out_shape=jax.ShapeDtypeStruct((T, D), F32),
        compiler_params=_params("arbitrary", "arbitrary"),
        name="out_projection_residual",
    )(a, b, c, w, x, gate.reshape(B, 1, D))


def _shared_gu_kernel(h_ref, sg_ref, su_ref, o_ref):
    h = h_ref[...]
    g = jnp.dot(h, sg_ref[...], preferred_element_type=F32)
    u = jnp.dot(h, su_ref[...], preferred_element_type=F32)
    o_ref[...] = (_silu(g) * u).astype(o_ref.dtype)


def _shared_gate_up(h, sg, su, tm=512):
    T, D = h.shape
    Fs = sg.shape[1]
    return pl.pallas_call(
        _shared_gu_kernel,
        grid=(T // tm,),
        in_specs=[pl.BlockSpec((tm, D), lambda i: (i, 0)),
                  pl.BlockSpec((D, Fs), lambda i: (0, 0)),
                  pl.BlockSpec((D, Fs), lambda i: (0, 0))],
        out_specs=pl.BlockSpec((tm, Fs), lambda i: (i, 0)),
        out_shape=jax.ShapeDtypeStruct((T, Fs), BF16),
        compiler_params=_params("arbitrary"),
        name="shared_gate_up",
    )(h, sg, su)


def _row_gather_start(idx_ref, n_rows, src_hbm, dst, sem):
    def body(r, carry):
        t = idx_ref[0, 0, r]
        pltpu.make_async_copy(src_hbm.at[pl.ds(t, 1), :], dst.at[pl.ds(r, 1), :], sem).start()
        return carry
    lax.fori_loop(0, n_rows, body, 0, unroll=8)


def _expert_kernel(be_ref, nact_ref, idx0_ref, idxn_ref, h_hbm, wg_ref, wu_ref, wd_ref, y_ref, xbuf, sem, *, tm):
    i = pl.program_id(0)
    nact = nact_ref[0]
    slot = i % 2

    @pl.when(i == 0)
    def _():
        _row_gather_start(idx0_ref, tm, h_hbm, xbuf.at[0], sem.at[0])

    @pl.when(i + 1 < nact)
    def _():
        _row_gather_start(idxn_ref, tm, h_hbm, xbuf.at[1 - slot], sem.at[1 - slot])

    @pl.when(i < nact)
    def _():
        pltpu.make_async_copy(h_hbm.at[pl.ds(0, tm), :], xbuf.at[slot], sem.at[slot]).wait()
        x = xbuf[slot].astype(BF16)
        g = jnp.dot(x, wg_ref[0], preferred_element_type=F32)
        u = jnp.dot(x, wu_ref[0], preferred_element_type=F32)
        hid = (_silu(g) * u).astype(BF16)
        y_ref[...] = jnp.dot(hid, wd_ref[0], preferred_element_type=F32)

    @pl.when(i >= nact)
    def _():
        y_ref[...] = jnp.zeros(y_ref.shape, y_ref.dtype)


def _routed_experts(h32, row_t, blk_e, nact, wg, wu, wd, tm):
    T, D = h32.shape
    E, _, Fe = wg.shape
    nblk = row_t.shape[0] // tm
    idx = row_t.reshape(nblk, 1, tm)
    grid_spec = pltpu.PrefetchScalarGridSpec(
        num_scalar_prefetch=2,
        grid=(nblk,),
        in_specs=[
            pl.BlockSpec((1, 1, tm), lambda i, be, na: (0, 0, 0), memory_space=pltpu.SMEM),
            pl.BlockSpec((1, 1, tm), lambda i, be, na: (jnp.minimum(i + 1, nblk - 1), 0, 0),
                         memory_space=pltpu.SMEM),
            pl.BlockSpec(memory_space=pl.ANY),
            pl.BlockSpec((1, D, Fe), lambda i, be, na: (be[i], 0, 0)),
            pl.BlockSpec((1, D, Fe), lambda i, be, na: (be[i], 0, 0)),
            pl.BlockSpec((1, Fe, D), lambda i, be, na: (be[i], 0, 0)),
        ],
        out_specs=pl.BlockSpec((tm, D), lambda i, be, na: (i, 0)),
        scratch_shapes=[pltpu.VMEM((2, tm, D), F32), pltpu.SemaphoreType.DMA((2,))],
    )
    return pl.pallas_call(
        functools.partial(_expert_kernel, tm=tm),
        grid_spec=grid_spec,
        out_shape=jax.ShapeDtypeStruct((nblk * tm, D), F32),
        compiler_params=_params("arbitrary"),
        name="routed_experts",
    )(blk_e, nact, idx, idx, h32, wg, wu, wd)


def _combine_kernel(pos0_ref, posn_ref, w_ref, hs_ref, sd_ref, x_ref, g_ref, y_hbm, o_ref, buf, sem, *, tc, top_k):
    i = pl.program_id(0)
    n = pl.num_programs(0)
    slot = i % 2

    def start(pos_ref, s):
        for k in range(top_k):
            def body(r, carry):
                p = pos_ref[0, 0, k * tc + r]
                pltpu.make_async_copy(y_hbm.at[pl.ds(p, 1), :], buf.at[s, k, pl.ds(r, 1), :], sem.at[s]).start()
                return carry
            lax.fori_loop(0, tc, body, 0, unroll=8)

    @pl.when(i == 0)
    def _():
        start(pos0_ref, 0)

    @pl.when(i + 1 < n)
    def _():
        start(posn_ref, 1 - slot)

    for k in range(top_k):
        pltpu.make_async_copy(y_hbm.at[pl.ds(0, tc), :], buf.at[slot, k], sem.at[slot]).wait()
    w = w_ref[...]
    routed = buf[slot, 0] * w[:, 0:1]
    for k in range(1, top_k):
        routed = routed + buf[slot, k] * w[:, k:k + 1]
    shared = jnp.dot(hs_ref[...], sd_ref[...], preferred_element_type=F32)
    o_ref[...] = x_ref[...] + g_ref[0] * (routed + shared)


def _moe_combine(y, pos, w, hs, sd, x, gate, seq, tc=128):
    T, D = x.shape
    B = gate.shape[0]
    K = pos.shape[1]
    Fs = hs.shape[1]
    nt = T // tc
    per_b = seq // tc
    pos_tiles = pos.reshape(nt, tc, K).transpose(0, 2, 1).reshape(nt, 1, K * tc)
    return pl.pallas_call(
        functools.partial(_combine_kernel, tc=tc, top_k=K),
        grid=(nt,),
        in_specs=[
            pl.BlockSpec((1, 1, K * tc), lambda i: (0, 0, 0), memory_space=pltpu.SMEM),
            pl.BlockSpec((1, 1, K * tc), lambda i: (jnp.minimum(i + 1, nt - 1), 0, 0), memory_space=pltpu.SMEM),
            pl.BlockSpec((tc, K), lambda i: (i, 0)),
            pl.BlockSpec((tc, Fs), lambda i: (i, 0)),
            pl.BlockSpec((Fs, D), lambda i: (0, 0)),
            pl.BlockSpec((tc, D), lambda i: (i, 0)),
            pl.BlockSpec((1, 1, D), lambda i: (i // per_b, 0, 0)),
            pl.BlockSpec(memory_space=pl.ANY),
        ],
        out_specs=pl.BlockSpec((tc, D), lambda i: (i, 0)),
        out_shape=jax.ShapeDtypeStruct((T, D), F32),
        scratch_shapes=[pltpu.VMEM((2, K, tc, D), F32), pltpu.SemaphoreType.DMA((2,))],
        compiler_params=_params("arbitrary"),
        name="moe_combine_residual",
    )(pos_tiles, pos_tiles, w, hs, sd, x, gate.reshape(B, 1, D), y)


def _route(logits, router_b):
    T = logits.shape[0]
    scores = jax.nn.sigmoid(logits)
    biased = scores + router_b.astype(F32)
    per_group = N_EXPERTS // N_GROUPS
    gscore = lax.top_k(biased.reshape(T, N_GROUPS, per_group), 2)[0].sum(-1)
    _, gidx = lax.top_k(gscore, TOPK_GROUPS)
    gmask = (gidx[:, :, None] == jnp.arange(N_GROUPS)[None, None, :]).any(1)
    emask = jnp.repeat(gmask, per_group, axis=1)
    _, eidx = lax.top_k(jnp.where(emask, biased, -jnp.inf), TOP_K)
    w = jnp.take_along_axis(scores, eidx, axis=1)
    w = w / w.sum(-1, keepdims=True) * ROUTED_SCALE
    return eidx, w


def _rank_kernel(e_ref, u_ref, rank_ref, cnt_ref, carry_sc):
    @pl.when(pl.program_id(0) == 0)
    def _():
        carry_sc[...] = jnp.zeros(carry_sc.shape, F32)

    e = e_ref[0]
    ta = e.shape[1]
    hit = lax.broadcasted_iota(I32, (LANES, ta), 0) == e
    onehot = jnp.where(hit, 1.0, 0.0).astype(BF16)
    earlier = jnp.dot(onehot, u_ref[...], preferred_element_type=F32)
    carry = carry_sc[...]
    rank = jnp.sum(jnp.where(hit, earlier + carry[:, 0:1], 0.0), axis=0, keepdims=True)
    rank_ref[0] = rank.astype(I32)
    carry = carry + jnp.dot(onehot, jnp.ones((ta, LANES), BF16), preferred_element_type=F32)
    carry_sc[...] = carry
    cnt_ref[...] = carry


def _expert_ranks(flat_e, ta=1024):
    A = flat_e.shape[0]
    nt = A // ta
    strictly_earlier = jnp.triu(jnp.ones((ta, ta), BF16), 1)
    rank, cnt = pl.pallas_call(
        _rank_kernel,
        grid=(nt,),
        in_specs=[pl.BlockSpec((1, 1, ta), lambda i: (i, 0, 0)),
                  pl.BlockSpec((ta, ta), lambda i: (0, 0))],
        out_specs=[pl.BlockSpec((1, 1, ta), lambda i: (i, 0, 0)),
                   pl.BlockSpec((LANES, LANES), lambda i: (0, 0))],
        out_shape=[jax.ShapeDtypeStruct((nt, 1, ta), I32), jax.ShapeDtypeStruct((LANES, LANES), F32)],
        scratch_shapes=[pltpu.VMEM((LANES, LANES), F32)],
        compiler_params=_params("arbitrary"),
        name="expert_ranks",
    )(flat_e.reshape(nt, 1, ta), strictly_earlier)
    return rank.reshape(A), cnt[:N_EXPERTS, 0].astype(I32)


def _dispatch_plan(eidx, tm):
    T, K = eidx.shape
    A = T * K
    E = N_EXPERTS
    flat_e = eidx.reshape(A).astype(I32)
    rank, counts = _expert_ranks(flat_e)
    padded = (counts + tm - 1) // tm * tm
    pad_end = jnp.cumsum(padded)
    pad_start = pad_end - padded
    start_of = jnp.sum(jnp.where(flat_e[:, None] == jnp.arange(E, dtype=I32)[None, :], pad_start[None, :], 0), axis=1)
    dest = start_of + rank
    nblk = A // tm + E
    row_t = jnp.zeros((nblk * tm,), I32).at[dest].set(jnp.arange(A, dtype=I32) // K, unique_indices=True)
    pos = dest.reshape(T, K)
    nact = (pad_end[-1] // tm).astype(I32)
    blk = jnp.arange(nblk, dtype=I32)
    blk_e = jnp.clip(jnp.searchsorted(pad_end, blk * tm, side='right'), 0, E - 1).astype(I32)
    blk_e = blk_e[jnp.minimum(blk, nact - 1)]
    return row_t, blk_e, nact.reshape(1), pos


def kernel(x, c, ada_w, ada_b, ada_table, norm1_g, w_in, swa_sink, mlstm_conv_w, mlstm_conv_b, mlstm_gate_b,
           mlstm_norm_g, diff_lambda, diff_norm_g, w_out, norm2_g, router_w, router_b, exp_gate, exp_up,
           exp_down, sh_gate, sh_up, sh_down, final_g):
    B, S, D = x.shape
    depth = w_in.shape[0]
    T = B * S
    d = HEAD_DIM
    n_swa_q = swa_sink.shape[1]
    n_swa_kv = n_swa_q // SWA_GROUP
    n_ml = mlstm_gate_b.shape[1] // 4
    n_diff = (D // d) - n_swa_q - n_ml
    swa_q_w, swa_kv_w, ml_w, diff_w = n_swa_q * d, n_swa_kv * d, n_ml * d, n_diff * d
    gate_w = 4 * n_ml
    gate_off = swa_q_w + 2 * swa_kv_w + 4 * ml_w
    k_a, v_a = swa_q_w, swa_q_w + swa_kv_w
    q_m = swa_q_w + 2 * swa_kv_w
    q_c = gate_off
    tn = 512
    diff_q_scale = (d // 2) ** -0.5 * math.log2(math.e)
    tiles = ((swa_q_w + swa_kv_w) // tn, q_c // tn, (q_c + diff_w) // tn, (q_c + 2 * diff_w) // tn, diff_q_scale)

    cos_a, sin_a = _rope_tables(S, d // 2, tn)
    cos_c, sin_c = _rope_tables(S, d // 4, tn)
    rope_tabs = (cos_a, sin_a, cos_c, sin_c)

    mod = _modulation(c, ada_w, ada_b, ada_table)
    xt = x.reshape(T, D)
    for l in range(depth):
        sh_a, sc_a, g_a, sh_f, sc_f, g_f = [mod[l, :, i] for i in range(ADA_CHUNKS)]
        w_main = jnp.concatenate([w_in[l, :, :gate_off], w_in[l, :, gate_off + gate_w:]], axis=1).astype(BF16)
        w_gate = jnp.pad(w_in[l, :, gate_off:gate_off + gate_w], ((0, 0), (0, LANES - gate_w)))
        h, gates = _norm_mod(xt, norm1_g[l], sc_a, sh_a, w_gate, S, emit_f32=False)
        proj = _in_projection(h, w_main, rope_tabs, S, tiles, tn=tn)
        out_a = _swa_attention(proj, swa_sink[l], B, S, n_swa_kv, 0, k_a, v_a)
        g4 = gates[:, :gate_w].reshape(B, S, 4, n_ml)
        out_b = _mlstm(proj, g4.transpose(0, 3, 2, 1), mlstm_gate_b[l],
                       mlstm_conv_w[l], mlstm_conv_b[l], mlstm_norm_g[l], B, S, n_ml, q_m)
        out_c = _diff_attention(proj, diff_lambda[l], diff_norm_g[l], B, S, n_diff, q_c, l)
        xt = _out_projection(out_a, out_b, out_c, w_out[l].astype(BF16), xt, g_a, S)

        w_router = jnp.pad(router_w[l], ((0, 0), (0, LANES - N_EXPERTS)))
        h, h32, logits = _norm_mod(xt, norm2_g[l], sc_f, sh_f, w_router, S, emit_f32=True)
        eidx, wts = _route(logits[:, :N_EXPERTS], router_b[l])
        tm = 256
        row_t, blk_e, nact, pos = _dispatch_plan(eidx, tm)
        y = _routed_experts(h32, row_t, blk_e, nact, exp_gate[l].astype(BF16), exp_up[l].astype(BF16),
                            exp_down[l].astype(BF16), tm)
        hs = _shared_gate_up(h, sh_gate[l].astype(BF16), sh_up[l].astype(BF16))
        xt = _moe_combine(y, pos, wts, hs, sh_down[l].astype(BF16), xt, g_f, S)
    return _final_norm(xt, final_g).reshape(B, S, D)
```

```python
import functools
import math

import jax
import jax.numpy as jnp
from jax import lax
from jax.experimental import pallas as pl
from jax.experimental.pallas import tpu as pltpu

F32 = jnp.float32
BF16 = jnp.bfloat16
I32 = jnp.int32

HEAD_DIM = 128
SWA_GROUP = 3
WINDOW = 128
MLSTM_CHUNK = 128
CONV_WIDTH = 5
ROPE_THETA = 10000.0
N_EXPERTS = 48
TOP_K = 6
N_GROUPS = 8
TOPK_GROUPS = 4
ROUTED_SCALE = 2.5
ADA_CHUNKS = 6
EPS = 1e-6

LANES = 128
BF16_SUBLANES = 16
MXU_WIDTH_V7X = 256
DIFF_QUERY_STRIP = MXU_WIDTH_V7X // 2
VMEM_LIMIT_V7X = 56 * 1024 * 1024

NT_DIMS = (((1,), (1,)), ((), ()))
TN_DIMS = (((0,), (0,)), ((), ()))


def _params(*sem):
    return pltpu.CompilerParams(dimension_semantics=sem, vmem_limit_bytes=VMEM_LIMIT_V7X)


def _silu(x):
    return x * jax.nn.sigmoid(x)


def _mod_kernel(c_ref, w_ref, b_ref, tab_ref, o_ref):
    a = _silu(c_ref[...])
    cond = jnp.dot(a.astype(BF16), w_ref[...].astype(BF16), preferred_element_type=F32) + b_ref[...]
    for l in range(tab_ref.shape[0]):
        o_ref[l] = cond + tab_ref[l]


def _modulation(c, ada_w, ada_b, ada_table, tn=512):
    B, D = c.shape
    depth = ada_table.shape[0]
    N = ada_w.shape[1]
    rows = 8
    c_pad = jnp.pad(c, ((0, rows - B), (0, 0)))
    out = pl.pallas_call(
        _mod_kernel,
        grid=(N // tn,),
        in_specs=[
            pl.BlockSpec((rows, D), lambda j: (0, 0)),
            pl.BlockSpec((D, tn), lambda j: (0, j)),
            pl.BlockSpec((1, tn), lambda j: (0, j)),
            pl.BlockSpec((depth, 1, tn), lambda j: (0, 0, j)),
        ],
        out_specs=pl.BlockSpec((depth, rows, tn), lambda j: (0, 0, j)),
        out_shape=jax.ShapeDtypeStruct((depth, rows, N), F32),
        compiler_params=_params("arbitrary"),
        name="adaln_mod",
    )(c_pad, ada_w, ada_b.reshape(1, N), ada_table.reshape(depth, 1, N))
    return out[:, :B].reshape(depth, B, ADA_CHUNKS, D)


def _norm_kernel(x_ref, g_ref, sc_ref, sh_ref, ws_ref, *out_refs, emit_f32):
    x = x_ref[...]
    ms = jnp.mean(x * x, axis=-1, keepdims=True)
    h = x * lax.rsqrt(ms + EPS) * g_ref[...]
    h = h * (1.0 + sc_ref[0]) + sh_ref[0]
    out_refs[0][...] = h.astype(BF16)
    if emit_f32:
        out_refs[1][...] = h
    out_refs[-1][...] = jnp.dot(h, ws_ref[...], preferred_element_type=F32,
                                precision=lax.Precision.HIGHEST).T


def _norm_mod(x, g, scale, shift, w_small, seq, emit_f32, tr=256):
    T, D = x.shape
    B = scale.shape[0]
    ns = w_small.shape[1]
    per_b = seq // tr
    row = lambda i: (i, 0)
    out_shape = [jax.ShapeDtypeStruct((T, D), BF16)]
    out_specs = [pl.BlockSpec((tr, D), row)]
    if emit_f32:
        out_shape.append(jax.ShapeDtypeStruct((T, D), F32))
        out_specs.append(pl.BlockSpec((tr, D), row))
    out_shape.append(jax.ShapeDtypeStruct((ns, T), F32))
    out_specs.append(pl.BlockSpec((ns, tr), lambda i: (0, i)))
    return pl.pallas_call(
        functools.partial(_norm_kernel, emit_f32=emit_f32),
        grid=(T // tr,),
        in_specs=[
            pl.BlockSpec((tr, D), row),
            pl.BlockSpec((1, D), lambda i: (0, 0)),
            pl.BlockSpec((1, 1, D), lambda i: (i // per_b, 0, 0)),
            pl.BlockSpec((1, 1, D), lambda i: (i // per_b, 0, 0)),
            pl.BlockSpec((D, ns), lambda i: (0, 0)),
        ],
        out_specs=out_specs,
        out_shape=out_shape,
        compiler_params=_params("arbitrary"),
        name="rmsnorm_adaln",
    )(x, g.reshape(1, D), scale.reshape(B, 1, D), shift.reshape(B, 1, D), w_small)


def _final_norm_kernel(x_ref, g_ref, o_ref):
    x = x_ref[...]
    ms = jnp.mean(x * x, axis=-1, keepdims=True)
    o_ref[...] = x * lax.rsqrt(ms + EPS) * g_ref[...]


def _final_norm(x, g, tr=256):
    T, D = x.shape
    return pl.pallas_call(
        _final_norm_kernel,
        grid=(T // tr,),
        in_specs=[pl.BlockSpec((tr, D), lambda i: (i, 0)), pl.BlockSpec((1, D), lambda i: (0, 0))],
        out_specs=pl.BlockSpec((tr, D), lambda i: (i, 0)),
        out_shape=jax.ShapeDtypeStruct((T, D), F32),
        compiler_params=_params("arbitrary"),
        name="final_rmsnorm",
    )(x, g.reshape(1, D))


def _rope_tile(acc, cos, sin, half):
    n = acc.shape[1]
    lane = lax.broadcasted_iota(I32, acc.shape, 1)
    from_right = pltpu.roll(acc, n - half, 1)
    from_left = pltpu.roll(acc, half, 1)
    rot = jnp.where((lane & (2 * half - 1)) < half, from_right, from_left)
    return acc * cos + rot * sin


def _inproj_kernel(a_ref, w_ref, cos_a, sin_a, cos_c, sin_c, o_ref, *, tiles):
    n_aqk, c_q0, c_k0, c_v0, q_scale = tiles
    j = pl.program_id(1)
    acc = jnp.dot(a_ref[...], w_ref[...], preferred_element_type=F32)

    @pl.when(j < n_aqk)
    def _():
        o_ref[...] = _rope_tile(acc, cos_a[...], sin_a[...], HEAD_DIM // 2).astype(o_ref.dtype)

    @pl.when(jnp.logical_and(j >= c_q0, j < c_k0))
    def _():
        o_ref[...] = (_rope_tile(acc, cos_c[...], sin_c[...], HEAD_DIM // 4) * q_scale).astype(o_ref.dtype)

    @pl.when(jnp.logical_and(j >= c_k0, j < c_v0))
    def _():
        o_ref[...] = _rope_tile(acc, cos_c[...], sin_c[...], HEAD_DIM // 4).astype(o_ref.dtype)

    @pl.when(jnp.logical_or(jnp.logical_and(j >= n_aqk, j < c_q0), j >= c_v0))
    def _():
        o_ref[...] = acc.astype(o_ref.dtype)


def _in_projection(h, w, rope_tabs, seq, tiles, tm=512, tn=512):
    T, D = h.shape
    N = w.shape[1]
    per_b = seq // tm
    tab = pl.BlockSpec((tm, tn), lambda i, j: (i % per_b, 0))
    return pl.pallas_call(
        functools.partial(_inproj_kernel, tiles=tiles),
        grid=(T // tm, N // tn),
        in_specs=[
            pl.BlockSpec((tm, D), lambda i, j: (i, 0)),
            pl.BlockSpec((D, tn), lambda i, j: (0, j)),
            tab, tab, tab, tab,
        ],
        out_specs=pl.BlockSpec((tm, tn), lambda i, j: (i, j)),
        out_shape=jax.ShapeDtypeStruct((T, N), BF16),
        compiler_params=_params("arbitrary", "arbitrary"),
        name="in_projection_rope",
    )(h, w, *rope_tabs)


def _rope_tables(seq, half, width):
    inv = jnp.power(ROPE_THETA, -jnp.arange(half, dtype=F32) / half)
    ang = jnp.arange(seq, dtype=F32)[:, None] * inv[None, :]
    cos, sin = jnp.cos(ang), jnp.sin(ang)
    reps = width // (2 * half)
    return (jnp.tile(jnp.concatenate([cos, cos], axis=1), (1, reps)),
            jnp.tile(jnp.concatenate([-sin, sin], axis=1), (1, reps)))


def _swa_kernel(sink_ref, q_ref, kp_ref, km_ref, kn_ref, vp_ref, vm_ref, vn_ref, o_ref, *, tq, seq):
    hk = pl.program_id(1)
    i = pl.program_id(2)
    W, G, d = WINDOW, SWA_GROUP, HEAD_DIM
    kcat = jnp.concatenate([kp_ref[...], km_ref[...], kn_ref[...]], axis=0)
    vcat = jnp.concatenate([vp_ref[...], vm_ref[...], vn_ref[...]], axis=0)
    qi = lax.broadcasted_iota(I32, (G * W, 3 * W), 0) & (W - 1)
    kj = lax.broadcasted_iota(I32, (G * W, 3 * W), 1)
    scale = d ** -0.5
    for r in range(tq // W):
        kw = kcat[r * W:(r + 3) * W]
        vw = vcat[r * W:(r + 3) * W]
        qs = jnp.concatenate([q_ref[r * W:(r + 1) * W, g * d:(g + 1) * d] for g in range(G)], axis=0)
        s = lax.dot_general(qs, kw, NT_DIMS, preferred_element_type=F32) * scale
        first = i * tq + (r - 1) * W
        lo = jnp.maximum(qi, -first)
        hi = jnp.minimum(qi + 2 * W, seq - 1 - first)
        s = jnp.where(kj >= lo, jnp.where(kj <= hi, s, -jnp.inf), -jnp.inf)
        sk = jnp.concatenate([jnp.full((W, 1), sink_ref[hk * G + g], F32) for g in range(G)], axis=0)
        mx = jnp.maximum(jnp.max(s, axis=-1, keepdims=True), sk)
        p = jnp.exp(s - mx)
        denom = jnp.sum(p, axis=-1, keepdims=True) + jnp.exp(sk - mx)
        o = jnp.dot(p.astype(BF16), vw, preferred_element_type=F32) / denom
        for g in range(G):
            o_ref[r * W:(r + 1) * W, g * d:(g + 1) * d] = o[g * W:(g + 1) * W].astype(o_ref.dtype)


def _swa_attention(proj, sink, batch, seq, n_kv, q_col, k_col, v_col, tq=512):
    T = proj.shape[0]
    W, G, d = WINDOW, SWA_GROUP, HEAD_DIM
    nq = seq // tq
    wpt = tq // W
    nw = seq // W
    qw = G * d
    kc, vc = k_col // d, v_col // d

    def prev_map(col):
        return lambda b, h, i, *_: (b * nw + jnp.maximum(i * wpt - 1, 0), col + h)

    def next_map(col):
        return lambda b, h, i, *_: (b * nw + jnp.minimum((i + 1) * wpt, nw - 1), col + h)

    def main_map(col):
        return lambda b, h, i, *_: (b * nq + i, col + h)

    grid_spec = pltpu.PrefetchScalarGridSpec(
        num_scalar_prefetch=1,
        grid=(batch, n_kv, nq),
        in_specs=[
            pl.BlockSpec((tq, qw), lambda b, h, i, *_: (b * nq + i, q_col // qw + h)),
            pl.BlockSpec((W, d), prev_map(kc)),
            pl.BlockSpec((tq, d), main_map(kc)),
            pl.BlockSpec((W, d), next_map(kc)),
            pl.BlockSpec((W, d), prev_map(vc)),
            pl.BlockSpec((tq, d), main_map(vc)),
            pl.BlockSpec((W, d), next_map(vc)),
        ],
        out_specs=pl.BlockSpec((tq, qw), lambda b, h, i, *_: (b * nq + i, h)),
    )
    return pl.pallas_call(
        functools.partial(_swa_kernel, tq=tq, seq=seq),
        grid_spec=grid_spec,
        out_shape=jax.ShapeDtypeStruct((T, n_kv * qw), BF16),
        compiler_params=_params("arbitrary", "arbitrary", "arbitrary"),
        name="swa_attention",
    )(sink.astype(F32), proj, proj, proj, proj, proj, proj, proj)


def _log_sigmoid(x):
    return jnp.minimum(x, 0.0) - jnp.log(1.0 + jnp.exp(-jnp.abs(x)))


def _split3_bf16(x):
    hi = x.astype(BF16)
    r = x - hi.astype(F32)
    mid = r.astype(BF16)
    lo = (r - mid.astype(F32)).astype(BF16)
    return hi, mid, lo


def _mlstm_kernel(gb_ref, q_ref, k_ref, v_ref, og_ref, gr_ref, cwq_ref, cwk_ref, cbq_ref, cbk_ref, ng_ref, o_ref,
                  qs, kst, vaug, hf, hb, rmat, *, seq, n_heads, hp):
    pair = pl.program_id(1)
    L, d, halo = MLSTM_CHUNK, HEAD_DIM, BF16_SUBLANES
    nc = seq // L
    pad = (CONV_WIDTH - 1) // 2
    w = hp * d

    tt = lax.broadcasted_iota(I32, (L, L), 0)
    ss = lax.broadcasted_iota(I32, (L, L), 1)
    for rev in (False, True):
        incl_t = (tt >= ss) if rev else (tt <= ss)
        rmat[int(rev), 0:L, :] = jnp.ones((L, L), BF16)
        rmat[int(rev), L:2 * L, :] = jnp.where(incl_t, -1.0, 0.0).astype(BF16)

    def conv_body(c, carry):
        base = pl.multiple_of(c * L, L)
        lo = pl.multiple_of(jnp.maximum(base - halo, 0), halo)
        hi = pl.multiple_of(jnp.minimum(base + L, seq - halo), halo)
        rows = lax.broadcasted_iota(I32, (L + 2 * halo, 1), 0) + (base - halo)

        def conv(src, w_ref, b_ref):
            x = jnp.concatenate([src[pl.ds(lo, halo), :], src[pl.ds(base, L), :], src[pl.ds(hi, halo), :]],
                                axis=0).astype(F32)
            x = jnp.where(rows >= 0, jnp.where(rows < seq, x, 0.0), 0.0)
            acc = jnp.zeros((L, w), F32) + b_ref[...]
            for j in range(CONV_WIDTH):
                start = halo + j - pad
                acc = acc + x[start:start + L] * w_ref[j:j + 1, :]
            return _silu(acc)

        qs[pl.ds(base, L), :] = conv(q_ref, cwq_ref, cbq_ref).astype(BF16)
        kc = conv(k_ref, cwk_ref, cbk_ref) * (d ** -0.5)
        for hh in range(hp):
            kst[hh * d:(hh + 1) * d, pl.ds(base, L)] = kc[:, hh * d:(hh + 1) * d].T.astype(BF16)
            vaug[pl.ds(base, L), 2 * hh * d:(2 * hh + 1) * d] = v_ref[pl.ds(base, L), hh * d:(hh + 1) * d]
            vaug[pl.ds(base, L), (2 * hh + 1) * d:(2 * hh + 2) * d] = jnp.ones((L, d), BF16)
        return carry

    lax.fori_loop(0, nc, conv_body, 0)

    def scan_chunk(c, carry, rev, hh):
        Caug, m = carry
        base = pl.multiple_of(c * L, L)
        head = pair * hp + hh
        gi, gf = (2, 3) if rev else (0, 1)
        q = qs[pl.ds(base, L), hh * d:(hh + 1) * d]
        kt = kst[hh * d:(hh + 1) * d, pl.ds(base, L)]
        va = vaug[pl.ds(base, L), 2 * hh * d:(2 * hh + 2) * d]
        i_row = gr_ref[hh, gi:gi + 1, pl.ds(base, L)] + gb_ref[gi * n_heads + head]
        f_row = _log_sigmoid(gr_ref[hh, gf:gf + 1, pl.ds(base, L)] + gb_ref[gf * n_heads + head])
        incl = (ss >= tt) if rev else (ss <= tt)
        dmat = jnp.zeros((L, L), F32)
        for part in _split3_bf16(f_row):
            fb = jnp.broadcast_to(part.astype(F32), (L, L))
            x = jnp.concatenate([jnp.where(incl, fb, 0.0), fb], axis=1).astype(BF16)
            dmat = dmat + jnp.dot(x, rmat[int(rev)], preferred_element_type=F32)
        first, last = (L - 1, 0) if rev else (0, L - 1)
        f_first = f_row[:, first:first + 1]
        b_col = dmat[:, first:first + 1] + f_first
        b_last = dmat[last:last + 1, first:first + 1] + f_first
        log_w = jnp.where(incl, dmat + i_row, -jnp.inf)
        log_inter = b_col + m
        m_t = jnp.maximum(log_inter, jnp.max(log_w, axis=-1, keepdims=True))
        w_intra = jnp.exp(log_w - m_t)
        w_inter = jnp.exp(log_inter - m_t)
        s = jnp.dot(q, kt, preferred_element_type=F32) * w_intra
        numden = (w_inter * jnp.dot(q, Caug.astype(BF16), preferred_element_type=F32)
                  + jnp.dot(s.astype(BF16), va, preferred_element_type=F32))
        inv = 1.0 / jnp.maximum(jnp.abs(numden[:, d:d + 1]), jnp.exp(-m_t))
        h_out = numden[:, 0:d] * inv
        log_ws = dmat[last:last + 1, :] + i_row
        m_new = jnp.maximum(b_last + m, jnp.max(log_ws, axis=-1, keepdims=True))
        decay = jnp.exp(b_last + m - m_new)
        kw = (kt.astype(F32) * jnp.exp(log_ws - m_new)).astype(BF16)
        Caug = decay * Caug + jnp.dot(kw, va, preferred_element_type=F32)
        return (Caug, m_new), h_out

    def scan_body(t, carries):
        out = []
        for hh in range(hp):
            for rev in (False, True):
                c = (nc - 1 - t) if rev else t
                carry, h_out = scan_chunk(c, carries[2 * hh + int(rev)], rev, hh)
                dst = hb if rev else hf
                dst[pl.ds(pl.multiple_of(c * L, L), L), hh * d:(hh + 1) * d] = h_out
                out.append(carry)
        return tuple(out)

    init = tuple((jnp.zeros((d, 2 * d), F32), jnp.zeros((1, 1), F32)) for _ in range(2 * hp))
    lax.fori_loop(0, nc, scan_body, init)

    slab = 4 * L

    def out_body(c, carry):
        base = pl.multiple_of(c * slab, slab)
        gate = jax.nn.sigmoid(og_ref[pl.ds(base, slab), :].astype(F32))
        for hh in range(hp):
            cols = slice(hh * d, (hh + 1) * d)
            h = hf[pl.ds(base, slab), cols] + hb[pl.ds(base, slab), cols]
            h = h * lax.rsqrt(jnp.mean(h * h, axis=-1, keepdims=True) + EPS) * ng_ref[:, cols]
            o_ref[pl.ds(base, slab), cols] = (h * gate[:, cols]).astype(o_ref.dtype)
        return carry

    lax.fori_loop(0, seq // slab, out_body, 0)


def _mlstm(proj, gates_row, gate_b, conv_w, conv_b, norm_g, batch, seq, n_heads, q_col, hp=2):
    T = proj.shape[0]
    d = HEAD_DIM
    width = n_heads * d
    w = hp * d
    c0 = q_col // w
    npair = n_heads // hp
    col = lambda off: (lambda b, h, *_: (b, c0 + off * npair + h))
    seq_blk = lambda off: pl.BlockSpec((seq, w), col(off))
    grid_spec = pltpu.PrefetchScalarGridSpec(
        num_scalar_prefetch=1,
        grid=(batch, npair),
        in_specs=[
            seq_blk(0), seq_blk(1), seq_blk(2), seq_blk(3),
            pl.BlockSpec((None, hp, 4, seq), lambda b, h, *_: (b, h, 0, 0)),
            pl.BlockSpec((CONV_WIDTH, w), lambda b, h, *_: (0, h)),
            pl.BlockSpec((CONV_WIDTH, w), lambda b, h, *_: (0, npair + h)),
            pl.BlockSpec((1, w), lambda b, h, *_: (0, h)),
            pl.BlockSpec((1, w), lambda b, h, *_: (0, npair + h)),
            pl.BlockSpec((1, w), lambda b, h, *_: (0, h)),
        ],
        out_specs=pl.BlockSpec((seq, w), lambda b, h, *_: (b, h)),
        scratch_shapes=[pltpu.VMEM((seq, w), BF16), pltpu.VMEM((w, seq), BF16), pltpu.VMEM((seq, 2 * w), BF16),
                        pltpu.VMEM((seq, w), F32), pltpu.VMEM((seq, w), F32),
                        pltpu.VMEM((2, 2 * MLSTM_CHUNK, MLSTM_CHUNK), BF16)],
    )
    return pl.pallas_call(
        functools.partial(_mlstm_kernel, seq=seq, n_heads=n_heads, hp=hp),
        grid_spec=grid_spec,
        out_shape=jax.ShapeDtypeStruct((T, width), BF16),
        compiler_params=_params("arbitrary", "arbitrary"),
        name="mlstm_bidir",
    )(gate_b.astype(F32), proj, proj, proj, proj, gates_row, conv_w, conv_w,
      conv_b.reshape(1, 2 * width), conv_b.reshape(1, 2 * width), norm_g.reshape(1, width))


def _diff_kernel(lam_ref, ng_ref, q_ref, k_ref, v_ref, o_ref, vt_sc, qt_sc, acc_sc, *, tq, tk, seq, lam_init):
    d = HEAD_DIM
    qs = DIFF_QUERY_STRIP

    @pl.when(pl.program_id(2) == 0)
    def _():
        for c in range(seq // tk):
            vt_sc[0:d, c * tk:(c + 1) * tk] = v_ref[c * tk:(c + 1) * tk, :].astype(F32).T.astype(BF16)
        vt_sc[d:, :] = jnp.ones((vt_sc.shape[0] - d, seq), BF16)

    qt = q_ref[...].astype(F32).T
    row = lax.broadcasted_iota(I32, (d, qs), 0)
    for s in range(tq // qs):
        qts = qt[:, s * qs:(s + 1) * qs]
        qt_sc[:, 2 * s * qs:(2 * s + 1) * qs] = jnp.where(row < d // 2, qts, 0.0).astype(BF16)
        qt_sc[:, (2 * s + 1) * qs:(2 * s + 2) * qs] = jnp.where(row >= d // 2, qts, 0.0).astype(BF16)

    lam = lam_ref[...]
    lam_full = (jnp.exp(jnp.sum(lam[0:1] * lam[1:2], axis=-1, keepdims=True))
                - jnp.exp(jnp.sum(lam[2:3] * lam[3:4], axis=-1, keepdims=True)) + lam_init)

    n_strips = tq // qs
    nk = seq // tk
    m = jnp.full((1, 2 * tq), -jnp.inf, F32)
    acc_sc[...] = jnp.zeros(acc_sc.shape, F32)
    scores = lambda j: jnp.dot(k_ref[j * tk:(j + 1) * tk, :], qt_sc[...], preferred_element_type=F32)
    st = scores(0)
    for j in range(nk):
        st_next = scores(j + 1) if j + 1 < nk else None
        m_new = jnp.maximum(m, jnp.max(st, axis=0, keepdims=True))
        p = jnp.exp2(st - m_new).astype(BF16)
        acc_sc[...] = jnp.exp2(m - m_new) * acc_sc[...] + jnp.dot(vt_sc[:, j * tk:(j + 1) * tk], p,
                                                                preferred_element_type=F32)
        m = m_new
        st = st_next
    for s in range(n_strips):
        acc = acc_sc[:, 2 * s * qs:(2 * s + 2) * qs]
        num = acc[0:d] / acc[d:d + 1]
        ot = num[:, 0:qs] - lam_full * num[:, qs:2 * qs]
        o = ot.T
        o = o * lax.rsqrt(jnp.mean(o * o, axis=-1, keepdims=True) + EPS) * ng_ref[...] * (1.0 - lam_init)
        o_ref[s * qs:(s + 1) * qs, :] = o.astype(o_ref.dtype)


def _diff_attention(proj, lam, norm_g, batch, seq, n_heads, q_col, layer_idx, tq=1024, tk=512):
    T = proj.shape[0]
    d = HEAD_DIM
    nq = seq // tq
    c0 = q_col // d
    lam_init = 0.8 - 0.6 * math.exp(-0.3 * layer_idx)
    return pl.pallas_call(
        functools.partial(_diff_kernel, tq=tq, tk=tk, seq=seq, lam_init=lam_init),
        grid=(batch, n_heads, nq),
        in_specs=[
            pl.BlockSpec(lam.shape, lambda b, h, i: (0, 0)),
            pl.BlockSpec((1, d), lambda b, h, i: (0, 0)),
            pl.BlockSpec((tq, d), lambda b, h, i: (b * nq + i, c0 + h)),
            pl.BlockSpec((seq, d), lambda b, h, i: (b, c0 + n_heads + h)),
            pl.BlockSpec((seq, d), lambda b, h, i: (b, c0 + 2 * n_heads + h)),
        ],
        out_specs=pl.BlockSpec((tq, d), lambda b, h, i: (b * nq + i, h)),
        out_shape=jax.ShapeDtypeStruct((T, n_heads * d), BF16),
        scratch_shapes=[pltpu.VMEM((d + BF16_SUBLANES, seq), BF16), pltpu.VMEM((d, 2 * tq), BF16),
                        pltpu.VMEM((d + BF16_SUBLANES, 2 * tq), F32)],
        compiler_params=_params("arbitrary", "arbitrary", "arbitrary"),
        name="diff_attention",
    )(lam.astype(F32), norm_g.reshape(1, d).astype(F32), proj, proj, proj)


def _outproj_kernel(a_ref, b_ref, c_ref, w_ref, x_ref, g_ref, o_ref):
    ka, kb = a_ref.shape[1], b_ref.shape[1]
    acc = jnp.dot(a_ref[...], w_ref[0:ka, :], preferred_element_type=F32)
    acc = acc + jnp.dot(b_ref[...], w_ref[ka:ka + kb, :], preferred_element_type=F32)
    acc = acc + jnp.dot(c_ref[...], w_ref[ka + kb:, :], preferred_element_type=F32)
    o_ref[...] = x_ref[...] + g_ref[0] * acc


def _out_projection(a, b, c, w, x, gate, seq, tm=512, tn=512):
    T, D = x.shape
    B = gate.shape[0]
    per_b = seq // tm
    K = w.shape[0]
    return pl.pallas_call(
        _outproj_kernel,
        grid=(T // tm, D // tn),
        in_specs=[
            pl.BlockSpec((tm, a.shape[1]), lambda i, j: (i, 0)),
            pl.BlockSpec((tm, b.shape[1]), lambda i, j: (i, 0)),
            pl.BlockSpec((tm, c.shape[1]), lambda i, j: (i, 0)),
            pl.BlockSpec((K, tn), lambda i, j: (0, j)),
            pl.BlockSpec((tm, tn), lambda i, j: (i, j)),
            pl.BlockSpec((1, 1, tn), lambda i, j: (i // per_b, 0, j)),
        ],
        out_specs=pl.BlockSpec((tm, tn), lambda i, j: (i, j)),
        out_shape=jax.ShapeDtypeStruct((T, D), F32),
        compiler_params=_params("arbitrary", "arbitrary"),
        name="out_projection_residual",
    )(a, b, c, w, x, gate.reshape(B, 1, D))


def _shared_gu_kernel(h_ref, sg_ref, su_ref, o_ref):
    h = h_ref[...]
    g = jnp.dot(h, sg_ref[...], preferred_element_type=F32)
    u = jnp.dot(h, su_ref[...], preferred_element_type=F32)
    o_ref[...] = (_silu(g) * u).astype(o_ref.dtype)


def _shared_gate_up(h, sg, su, tm=512):
    T, D = h.shape
    Fs = sg.shape[1]
    return pl.pallas_call(
        _shared_gu_kernel,
        grid=(T // tm,),
        in_specs=[pl.BlockSpec((tm, D), lambda i: (i, 0)),
                  pl.BlockSpec((D, Fs), lambda i: (0, 0)),
                  pl.BlockSpec((D, Fs), lambda i: (0, 0))],
        out_specs=pl.BlockSpec((tm, Fs), lambda i: (i, 0)),
        out_shape=jax.ShapeDtypeStruct((T, Fs), BF16),
        compiler_params=_params("arbitrary"),
        name="shared_gate_up",
    )(h, sg, su)


def _row_gather_start(idx_ref, n_rows, src_hbm, dst, sem):
    def body(r, carry):
        t = idx_ref[0, 0, r]
        pltpu.make_async_copy(src_hbm.at[pl.ds(t, 1), :], dst.at[pl.ds(r, 1), :], sem).start()
        return carry
    lax.fori_loop(0, n_rows, body, 0, unroll=8)


def _expert_kernel(be_ref, nact_ref, idx0_ref, idxn_ref, h_hbm, wg_ref, wu_ref, wd_ref, y_ref, xbuf, sem, *, tm):
    i = pl.program_id(0)
    nact = nact_ref[0]
    slot = i % 2

    @pl.when(i == 0)
    def _():
        _row_gather_start(idx0_ref, tm, h_hbm, xbuf.at[0], sem.at[0])

    @pl.when(i + 1 < nact)
    def _():
        _row_gather_start(idxn_ref, tm, h_hbm, xbuf.at[1 - slot], sem.at[1 - slot])

    @pl.when(i < nact)
    def _():
        pltpu.make_async_copy(h_hbm.at[pl.ds(0, tm), :], xbuf.at[slot], sem.at[slot]).wait()
        x = xbuf[slot].astype(BF16)
        g = jnp.dot(x, wg_ref[0], preferred_element_type=F32)
        u = jnp.dot(x, wu_ref[0], preferred_element_type=F32)
        hid = (_silu(g) * u).astype(BF16)
        y_ref[...] = jnp.dot(hid, wd_ref[0], preferred_element_type=F32)

    @pl.when(i >= nact)
    def _():
        y_ref[...] = jnp.zeros(y_ref.shape, y_ref.dtype)


def _routed_experts(h32, row_t, blk_e, nact, wg, wu, wd, tm):
    T, D = h32.shape
    E, _, Fe = wg.shape
    nblk = row_t.shape[0] // tm
    idx = row_t.reshape(nblk, 1, tm)
    grid_spec = pltpu.PrefetchScalarGridSpec(
        num_scalar_prefetch=2,
        grid=(nblk,),
        in_specs=[
            pl.BlockSpec((1, 1, tm), lambda i, be, na: (0, 0, 0), memory_space=pltpu.SMEM),
            pl.BlockSpec((1, 1, tm), lambda i, be, na: (jnp.minimum(i + 1, nblk - 1), 0, 0),
                         memory_space=pltpu.SMEM),
            pl.BlockSpec(memory_space=pl.ANY),
            pl.BlockSpec((1, D, Fe), lambda i, be, na: (be[i], 0, 0)),
            pl.BlockSpec((1, D, Fe), lambda i, be, na: (be[i], 0, 0)),
            pl.BlockSpec((1, Fe, D), lambda i, be, na: (be[i], 0, 0)),
        ],
        out_specs=pl.BlockSpec((tm, D), lambda i, be, na: (i, 0)),
        scratch_shapes=[pltpu.VMEM((2, tm, D), F32), pltpu.SemaphoreType.DMA((2,))],
    )
    return pl.pallas_call(
        functools.partial(_expert_kernel, tm=tm),
        grid_spec=grid_spec,
        out_shape=jax.ShapeDtypeStruct((nblk * tm, D), F32),
        compiler_params=_params("arbitrary"),
        name="routed_experts",
    )(blk_e, nact, idx, idx, h32, wg, wu, wd)


def _combine_kernel(pos0_ref, posn_ref, w_ref, hs_ref, sd_ref, x_ref, g_ref, y_hbm, o_ref, buf, sem, *, tc, top_k):
    i = pl.program_id(0)
    n = pl.num_programs(0)
    slot = i % 2

    def start(pos_ref, s):
        for k in range(top_k):
            def body(r, carry):
                p = pos_ref[0, 0, k * tc + r]
                pltpu.make_async_copy(y_hbm.at[pl.ds(p, 1), :], buf.at[s, k, pl.ds(r, 1), :], sem.at[s]).start()
                return carry
            lax.fori_loop(0, tc, body, 0, unroll=8)

    @pl.when(i == 0)
    def _():
        start(pos0_ref, 0)

    @pl.when(i + 1 < n)
    def _():
        start(posn_ref, 1 - slot)

    for k in range(top_k):
        pltpu.make_async_copy(y_hbm.at[pl.ds(0, tc), :], buf.at[slot, k], sem.at[slot]).wait()
    w = w_ref[...]
    routed = buf[slot, 0] * w[:, 0:1]
    for k in range(1, top_k):
        routed = routed + buf[slot, k] * w[:, k:k + 1]
    shared = jnp.dot(hs_ref[...], sd_ref[...], preferred_element_type=F32)
    o_ref[...] = x_ref[...] + g_ref[0] * (routed + shared)


def _moe_combine(y, pos, w, hs, sd, x, gate, seq, tc=128):
    T, D = x.shape
    B = gate.shape[0]
    K = pos.shape[0]
    Fs = hs.shape[1]
    nt = T // tc
    per_b = seq // tc
    pos_tiles = pos.reshape(K, nt, tc).transpose(1, 0, 2).reshape(nt, 1, K * tc)
    w = w.T
    return pl.pallas_call(
        functools.partial(_combine_kernel, tc=tc, top_k=K),
        grid=(nt,),
        in_specs=[
            pl.BlockSpec((1, 1, K * tc), lambda i: (0, 0, 0), memory_space=pltpu.SMEM),
            pl.BlockSpec((1, 1, K * tc), lambda i: (jnp.minimum(i + 1, nt - 1), 0, 0), memory_space=pltpu.SMEM),
            pl.BlockSpec((tc, K), lambda i: (i, 0)),
            pl.BlockSpec((tc, Fs), lambda i: (i, 0)),
            pl.BlockSpec((Fs, D), lambda i: (0, 0)),
            pl.BlockSpec((tc, D), lambda i: (i, 0)),
            pl.BlockSpec((1, 1, D), lambda i: (i // per_b, 0, 0)),
            pl.BlockSpec(memory_space=pl.ANY),
        ],
        out_specs=pl.BlockSpec((tc, D), lambda i: (i, 0)),
        out_shape=jax.ShapeDtypeStruct((T, D), F32),
        scratch_shapes=[pltpu.VMEM((2, K, tc, D), F32), pltpu.SemaphoreType.DMA((2,))],
        compiler_params=_params("arbitrary"),
        name="moe_combine_residual",
    )(pos_tiles, pos_tiles, w, hs, sd, x, gate.reshape(B, 1, D), y)


def _route_kernel(lt_ref, b_ref, u_ref, eid_ref, rnk_ref, w_ref, cnt_ref, carry_sc):
    G, P, K = N_GROUPS, N_EXPERTS // N_GROUPS, TOP_K
    E = G * P
    tr = lt_ref.shape[1]

    @pl.when(pl.program_id(0) == 0)
    def _():
        carry_sc[...] = jnp.zeros(carry_sc.shape, F32)

    one = lambda cond: jnp.where(cond, 1.0, 0.0)
    gidx = lax.broadcasted_iota(I32, (G, tr), 0)
    rows = lambda a, g: jnp.broadcast_to(a[g:g + 1, :], (G, tr))
    score = [jax.nn.sigmoid(lt_ref[j * G:(j + 1) * G, :]) for j in range(P)]
    biased = [score[j] + b_ref[j * G:(j + 1) * G, 0:1] for j in range(P)]

    top1, top2 = biased[0], jnp.full((G, tr), -jnp.inf, F32)
    for j in range(1, P):
        top2 = jnp.maximum(top2, jnp.minimum(top1, biased[j]))
        top1 = jnp.maximum(top1, biased[j])
    gscore = top1 + top2

    beaten = jnp.zeros((G, tr), F32)
    for c in range(G):
        vc = rows(gscore, c)
        beaten = beaten + one(vc > gscore) + jnp.where(gidx > c, one(vc == gscore), 0.0)
    keep = beaten < TOPK_GROUPS
    masked = [jnp.where(keep, biased[j], -jnp.inf) for j in range(P)]

    rank = [jnp.zeros((G, tr), F32) for _ in range(P)]
    for jc in range(P):
        for gc in range(G):
            vc = rows(masked[jc], gc)
            for j in range(P):
                lower_id = (gidx >= gc) if j > jc else (gidx > gc)
                rank[j] = rank[j] + one(vc > masked[j]) + jnp.where(lower_id, one(vc == masked[j]), 0.0)
    sel = [jnp.where(keep, one(rank[j] < K), 0.0) for j in range(P)]

    picked = sel[0] * score[0]
    for j in range(1, P):
        picked = picked + sel[j] * score[j]
    total = jnp.sum(picked, axis=0, keepdims=True)
    weight = [sel[j] * score[j] / total * ROUTED_SCALE for j in range(P)]

    sel_all = jnp.concatenate(sel, axis=0).astype(BF16)
    earlier = jnp.dot(sel_all, u_ref[...], preferred_element_type=F32)
    carry = carry_sc[...]
    place = [earlier[j * G:(j + 1) * G, :] + carry[j * G:(j + 1) * G, 0:1] for j in range(P)]
    carry_sc[0:E, :] = carry[0:E, :] + jnp.dot(sel_all, jnp.ones((tr, LANES), BF16), preferred_element_type=F32)
    cnt_ref[...] = carry_sc[...]

    eid = [(gidx * P + j).astype(F32) for j in range(P)]
    pick = lambda vals, hit: jnp.sum(sum(hit[j] * vals[j] for j in range(P)), axis=0, keepdims=True)
    out_e, out_r, out_w = [], [], []
    for k in range(K):
        hit = [sel[j] * one(rank[j] == k) for j in range(P)]
        out_e.append(pick(eid, hit))
        out_r.append(pick(place, hit))
        out_w.append(pick(weight, hit))
    fill = [jnp.zeros((eid_ref.shape[0] - K, tr), F32)]
    eid_ref[...] = jnp.concatenate(out_e + fill, axis=0).astype(I32)
    rnk_ref[...] = jnp.concatenate(out_r + fill, axis=0).astype(I32)
    w_ref[...] = jnp.concatenate(out_w + fill, axis=0)


def _route(logits_t, bias_col, tr=512):
    T = logits_t.shape[1]
    slots = 8
    strictly_earlier = jnp.triu(jnp.ones((tr, tr), BF16), 1)
    tile = lambda i: (0, i)
    eid, place, w, cnt = pl.pallas_call(
        _route_kernel,
        grid=(T // tr,),
        in_specs=[pl.BlockSpec((LANES, tr), tile),
                  pl.BlockSpec((LANES, 1), lambda i: (0, 0)),
                  pl.BlockSpec((tr, tr), lambda i: (0, 0))],
        out_specs=[pl.BlockSpec((slots, tr), tile), pl.BlockSpec((slots, tr), tile), pl.BlockSpec((slots, tr), tile),
                   pl.BlockSpec((LANES, LANES), lambda i: (0, 0))],
        out_shape=[jax.ShapeDtypeStruct((slots, T), I32), jax.ShapeDtypeStruct((slots, T), I32),
                   jax.ShapeDtypeStruct((slots, T), F32), jax.ShapeDtypeStruct((LANES, LANES), F32)],
        scratch_shapes=[pltpu.VMEM((LANES, LANES), F32)],
        compiler_params=_params("arbitrary"),
        name="moe_routing",
    )(logits_t, bias_col, strictly_earlier)
    return eid[:TOP_K], place[:TOP_K], w[:TOP_K], cnt[:N_EXPERTS, 0].astype(I32)


def _router_row_order(a):
    G, P = N_GROUPS, N_EXPERTS // N_GROUPS
    return a.reshape(a.shape[:-1] + (G, P)).swapaxes(-1, -2).reshape(a.shape[:-1] + (G * P,))


def _dispatch_plan(eid, place, counts_rows, tm):
    K, T = eid.shape
    E, G, P = N_EXPERTS, N_GROUPS, N_EXPERTS // N_GROUPS
    counts = counts_rows.reshape(P, G).T.reshape(E)
    padded = (counts + tm - 1) // tm * tm
    pad_end = jnp.cumsum(padded)
    pad_start = pad_end - padded
    ids = jnp.arange(E, dtype=I32)
    dest = place + jnp.sum(jnp.where(eid[..., None] == ids, pad_start, 0), axis=-1)
    nblk = K * T // tm + E
    tok = jnp.broadcast_to(jnp.arange(T, dtype=I32)[None, :], (K, T))
    row_t = jnp.zeros((nblk * tm,), I32).at[dest.reshape(-1)].set(tok.reshape(-1), unique_indices=True)
    nact = (pad_end[-1] // tm).astype(I32)
    blk = jnp.minimum(jnp.arange(nblk, dtype=I32), nact - 1)
    blk_e = jnp.sum((pad_end[None, :] <= (blk * tm)[:, None]).astype(I32), axis=1)
    return row_t, blk_e, nact.reshape(1), dest


def kernel(x, c, ada_w, ada_b, ada_table, norm1_g, w_in, swa_sink, mlstm_conv_w, mlstm_conv_b, mlstm_gate_b,
           mlstm_norm_g, diff_lambda, diff_norm_g, w_out, norm2_g, router_w, router_b, exp_gate, exp_up,
           exp_down, sh_gate, sh_up, sh_down, final_g):
    B, S, D = x.shape
    depth = w_in.shape[0]
    T = B * S
    d = HEAD_DIM
    n_swa_q = swa_sink.shape[1]
    n_swa_kv = n_swa_q // SWA_GROUP
    n_ml = mlstm_gate_b.shape[1] // 4
    n_diff = (D // d) - n_swa_q - n_ml
    swa_q_w, swa_kv_w, ml_w, diff_w = n_swa_q * d, n_swa_kv * d, n_ml * d, n_diff * d
    gate_w = 4 * n_ml
    gate_off = swa_q_w + 2 * swa_kv_w + 4 * ml_w
    k_a, v_a = swa_q_w, swa_q_w + swa_kv_w
    q_m = swa_q_w + 2 * swa_kv_w
    q_c = gate_off
    tn = 512
    diff_q_scale = (d // 2) ** -0.5 * math.log2(math.e)
    tiles = ((swa_q_w + swa_kv_w) // tn, q_c // tn, (q_c + diff_w) // tn, (q_c + 2 * diff_w) // tn, diff_q_scale)

    cos_a, sin_a = _rope_tables(S, d // 2, tn)
    cos_c, sin_c = _rope_tables(S, d // 4, tn)
    rope_tabs = (cos_a, sin_a, cos_c, sin_c)

    mod = _modulation(c, ada_w, ada_b, ada_table)
    xt = x.reshape(T, D)
    for l in range(depth):
        sh_a, sc_a, g_a, sh_f, sc_f, g_f = [mod[l, :, i] for i in range(ADA_CHUNKS)]
        w_main = jnp.concatenate([w_in[l, :, :gate_off], w_in[l, :, gate_off + gate_w:]], axis=1).astype(BF16)
        w_gate = jnp.pad(w_in[l, :, gate_off:gate_off + gate_w], ((0, 0), (0, LANES - gate_w)))
        h, gates = _norm_mod(xt, norm1_g[l], sc_a, sh_a, w_gate, S, emit_f32=False)
        proj = _in_projection(h, w_main, rope_tabs, S, tiles, tn=tn)
        out_a = _swa_attention(proj, swa_sink[l], B, S, n_swa_kv, 0, k_a, v_a)
        g4 = gates[:gate_w].reshape(4, n_ml, B, S)
        out_b = _mlstm(proj, g4.transpose(2, 1, 0, 3), mlstm_gate_b[l],
                       mlstm_conv_w[l], mlstm_conv_b[l], mlstm_norm_g[l], B, S, n_ml, q_m)
        out_c = _diff_attention(proj, diff_lambda[l], diff_norm_g[l], B, S, n_diff, q_c, l)
        xt = _out_projection(out_a, out_b, out_c, w_out[l].astype(BF16), xt, g_a, S)

        w_router = jnp.pad(_router_row_order(router_w[l]), ((0, 0), (0, LANES - N_EXPERTS)))
        b_router = jnp.pad(_router_row_order(router_b[l].astype(F32)), (0, LANES - N_EXPERTS)).reshape(LANES, 1)
        h, h32, logits_t = _norm_mod(xt, norm2_g[l], sc_f, sh_f, w_router, S, emit_f32=True)
        eid, place, wts, counts = _route(logits_t, b_router)
        tm = 256
        row_t, blk_e, nact, pos = _dispatch_plan(eid, place, counts, tm)
        y = _routed_experts(h32, row_t, blk_e, nact, exp_gate[l].astype(BF16), exp_up[l].astype(BF16),
                            exp_down[l].astype(BF16), tm)
        hs = _shared_gate_up(h, sh_gate[l].astype(BF16), sh_up[l].astype(BF16))
        xt = _moe_combine(y, pos, wts, hs, sh_down[l].astype(BF16), xt, g_f, S)
    return _final_norm(xt, final_g).reshape(B, S, D)
```

```python
import functools
import math

import jax
import jax.numpy as jnp
from jax import lax
from jax.experimental import pallas as pl
from jax.experimental.pallas import tpu as pltpu

F32 = jnp.float32
BF16 = jnp.bfloat16
I32 = jnp.int32

HEAD_DIM = 128
SWA_GROUP = 3
WINDOW = 128
MLSTM_CHUNK = 128
CONV_WIDTH = 5
ROPE_THETA = 10000.0
N_EXPERTS = 48
TOP_K = 6
N_GROUPS = 8
TOPK_GROUPS = 4
ROUTED_SCALE = 2.5
ADA_CHUNKS = 6
EPS = 1e-6

LANES = 128
BF16_SUBLANES = 16
MXU_WIDTH_V7X = 256
DIFF_QUERY_STRIP = MXU_WIDTH_V7X // 2
VMEM_LIMIT_V7X = 56 * 1024 * 1024

NT_DIMS = (((1,), (1,)), ((), ()))


def _params(*sem):
    return pltpu.CompilerParams(dimension_semantics=sem, vmem_limit_bytes=VMEM_LIMIT_V7X)


def _silu(x):
    return x * jax.nn.sigmoid(x)


def _mod_kernel(c_ref, w_ref, b_ref, tab_ref, o_ref):
    a = _silu(c_ref[...])
    cond = jnp.dot(a.astype(BF16), w_ref[...].astype(BF16), preferred_element_type=F32) + b_ref[...]
    for l in range(tab_ref.shape[0]):
        o_ref[l] = cond + tab_ref[l]


def _modulation(c, ada_w, ada_b, ada_table, tn=512):
    B, D = c.shape
    depth = ada_table.shape[0]
    N = ada_w.shape[1]
    rows = 8
    c_pad = jnp.pad(c, ((0, rows - B), (0, 0)))
    out = pl.pallas_call(
        _mod_kernel,
        grid=(N // tn,),
        in_specs=[
            pl.BlockSpec((rows, D), lambda j: (0, 0)),
            pl.BlockSpec((D, tn), lambda j: (0, j)),
            pl.BlockSpec((1, tn), lambda j: (0, j)),
            pl.BlockSpec((depth, 1, tn), lambda j: (0, 0, j)),
        ],
        out_specs=pl.BlockSpec((depth, rows, tn), lambda j: (0, 0, j)),
        out_shape=jax.ShapeDtypeStruct((depth, rows, N), F32),
        compiler_params=_params("arbitrary"),
        name="adaln_mod",
    )(c_pad, ada_w, ada_b.reshape(1, N), ada_table.reshape(depth, 1, N))
    return out[:, :B].reshape(depth, B, ADA_CHUNKS, D)


def _norm_kernel(x_ref, g_ref, sc_ref, sh_ref, ws_ref, *out_refs, emit_f32):
    x = x_ref[...]
    ms = jnp.mean(x * x, axis=-1, keepdims=True)
    h = x * lax.rsqrt(ms + EPS) * g_ref[...]
    h = h * (1.0 + sc_ref[0]) + sh_ref[0]
    out_refs[0][...] = h.astype(BF16)
    if emit_f32:
        out_refs[1][...] = h
    out_refs[-1][...] = jnp.dot(h, ws_ref[...], preferred_element_type=F32,
                                precision=lax.Precision.HIGHEST).T


def _norm_mod(x, g, scale, shift, w_small, seq, emit_f32, tr=256):
    T, D = x.shape
    B = scale.shape[0]
    ns = w_small.shape[1]
    per_b = seq // tr
    row = lambda i: (i, 0)
    out_shape = [jax.ShapeDtypeStruct((T, D), BF16)]
    out_specs = [pl.BlockSpec((tr, D), row)]
    if emit_f32:
        out_shape.append(jax.ShapeDtypeStruct((T, D), F32))
        out_specs.append(pl.BlockSpec((tr, D), row))
    out_shape.append(jax.ShapeDtypeStruct((ns, T), F32))
    out_specs.append(pl.BlockSpec((ns, tr), lambda i: (0, i)))
    return pl.pallas_call(
        functools.partial(_norm_kernel, emit_f32=emit_f32),
        grid=(T // tr,),
        in_specs=[
            pl.BlockSpec((tr, D), row),
            pl.BlockSpec((1, D), lambda i: (0, 0)),
            pl.BlockSpec((1, 1, D), lambda i: (i // per_b, 0, 0)),
            pl.BlockSpec((1, 1, D), lambda i: (i // per_b, 0, 0)),
            pl.BlockSpec((D, ns), lambda i: (0, 0)),
        ],
        out_specs=out_specs,
        out_shape=out_shape,
        compiler_params=_params("arbitrary"),
        name="rmsnorm_adaln",
    )(x, g.reshape(1, D), scale.reshape(B, 1, D), shift.reshape(B, 1, D), w_small)


def _rope_tile(acc, cos, sin, half):
    n = acc.shape[1]
    lane = lax.broadcasted_iota(I32, acc.shape, 1)
    from_right = pltpu.roll(acc, n - half, 1)
    from_left = pltpu.roll(acc, half, 1)
    rot = jnp.where((lane & (2 * half - 1)) < half, from_right, from_left)
    return acc * cos + rot * sin


def _inproj_kernel(a_ref, w_ref, cos_a, sin_a, cos_c, sin_c, o_ref, *, tiles):
    n_aqk, c_q0, c_k0, c_v0, q_scale = tiles
    j = pl.program_id(1)
    acc = jnp.dot(a_ref[...], w_ref[...], preferred_element_type=F32)

    @pl.when(j < n_aqk)
    def _():
        o_ref[...] = _rope_tile(acc, cos_a[...], sin_a[...], HEAD_DIM // 2).astype(o_ref.dtype)

    @pl.when(jnp.logical_and(j >= c_q0, j < c_k0))
    def _():
        o_ref[...] = (_rope_tile(acc, cos_c[...], sin_c[...], HEAD_DIM // 4) * q_scale).astype(o_ref.dtype)

    @pl.when(jnp.logical_and(j >= c_k0, j < c_v0))
    def _():
        o_ref[...] = _rope_tile(acc, cos_c[...], sin_c[...], HEAD_DIM // 4).astype(o_ref.dtype)

    @pl.when(jnp.logical_or(jnp.logical_and(j >= n_aqk, j < c_q0), j >= c_v0))
    def _():
        o_ref[...] = acc.astype(o_ref.dtype)


def _in_projection(h, w, rope_tabs, seq, tiles, tm=1024, tn=512):
    T, D = h.shape
    N = w.shape[1]
    per_b = seq // tm
    tab = pl.BlockSpec((tm, tn), lambda i, j: (i % per_b, 0))
    return pl.pallas_call(
        functools.partial(_inproj_kernel, tiles=tiles),
        grid=(T // tm, N // tn),
        in_specs=[
            pl.BlockSpec((tm, D), lambda i, j: (i, 0)),
            pl.BlockSpec((D, tn), lambda i, j: (0, j)),
            tab, tab, tab, tab,
        ],
        out_specs=pl.BlockSpec((tm, tn), lambda i, j: (i, j)),
        out_shape=jax.ShapeDtypeStruct((T, N), BF16),
        compiler_params=_params("arbitrary", "arbitrary"),
        name="in_projection_rope",
    )(h, w, *rope_tabs)


def _rope_tables(seq, half, width):
    inv = jnp.power(ROPE_THETA, -jnp.arange(half, dtype=F32) / half)
    ang = jnp.arange(seq, dtype=F32)[:, None] * inv[None, :]
    cos, sin = jnp.cos(ang), jnp.sin(ang)
    reps = width // (2 * half)
    return (jnp.tile(jnp.concatenate([cos, cos], axis=1), (1, reps)),
            jnp.tile(jnp.concatenate([-sin, sin], axis=1), (1, reps)))


def _swa_kernel(sink_ref, q_ref, kp_ref, km_ref, kn_ref, vp_ref, vm_ref, vn_ref, o_ref, *, tq, seq):
    hk = pl.program_id(1)
    i = pl.program_id(2)
    W, G, d = WINDOW, SWA_GROUP, HEAD_DIM
    kcat = jnp.concatenate([kp_ref[...], km_ref[...], kn_ref[...]], axis=0)
    vcat = jnp.concatenate([vp_ref[...], vm_ref[...], vn_ref[...]], axis=0)
    qi = lax.broadcasted_iota(I32, (G * W, 3 * W), 0) & (W - 1)
    kj = lax.broadcasted_iota(I32, (G * W, 3 * W), 1)
    scale = d ** -0.5
    for r in range(tq // W):
        kw = kcat[r * W:(r + 3) * W]
        vw = vcat[r * W:(r + 3) * W]
        qs = jnp.concatenate([q_ref[r * W:(r + 1) * W, g * d:(g + 1) * d] for g in range(G)], axis=0)
        s = lax.dot_general(qs, kw, NT_DIMS, preferred_element_type=F32) * scale
        first = i * tq + (r - 1) * W
        lo = jnp.maximum(qi, -first)
        hi = jnp.minimum(qi + 2 * W, seq - 1 - first)
        s = jnp.where(kj >= lo, jnp.where(kj <= hi, s, -jnp.inf), -jnp.inf)
        sk = jnp.concatenate([jnp.full((W, 1), sink_ref[hk * G + g], F32) for g in range(G)], axis=0)
        mx = jnp.maximum(jnp.max(s, axis=-1, keepdims=True), sk)
        p = jnp.exp(s - mx)
        denom = jnp.sum(p, axis=-1, keepdims=True) + jnp.exp(sk - mx)
        o = jnp.dot(p.astype(BF16), vw, preferred_element_type=F32) / denom
        for g in range(G):
            o_ref[r * W:(r + 1) * W, g * d:(g + 1) * d] = o[g * W:(g + 1) * W].astype(o_ref.dtype)


def _swa_attention(proj, sink, batch, seq, n_kv, q_col, k_col, v_col, tq=1024):
    T = proj.shape[0]
    W, G, d = WINDOW, SWA_GROUP, HEAD_DIM
    nq = seq // tq
    wpt = tq // W
    nw = seq // W
    qw = G * d
    kc, vc = k_col // d, v_col // d

    def prev_map(col):
        return lambda b, h, i, *_: (b * nw + jnp.maximum(i * wpt - 1, 0), col + h)

    def next_map(col):
        return lambda b, h, i, *_: (b * nw + jnp.minimum((i + 1) * wpt, nw - 1), col + h)

    def main_map(col):
        return lambda b, h, i, *_: (b * nq + i, col + h)

    grid_spec = pltpu.PrefetchScalarGridSpec(
        num_scalar_prefetch=1,
        grid=(batch, n_kv, nq),
        in_specs=[
            pl.BlockSpec((tq, qw), lambda b, h, i, *_: (b * nq + i, q_col // qw + h)),
            pl.BlockSpec((W, d), prev_map(kc)),
            pl.BlockSpec((tq, d), main_map(kc)),
            pl.BlockSpec((W, d), next_map(kc)),
            pl.BlockSpec((W, d), prev_map(vc)),
            pl.BlockSpec((tq, d), main_map(vc)),
            pl.BlockSpec((W, d), next_map(vc)),
        ],
        out_specs=pl.BlockSpec((tq, qw), lambda b, h, i, *_: (b * nq + i, h)),
    )
    return pl.pallas_call(
        functools.partial(_swa_kernel, tq=tq, seq=seq),
        grid_spec=grid_spec,
        out_shape=jax.ShapeDtypeStruct((T, n_kv * qw), BF16),
        compiler_params=_params("arbitrary", "arbitrary", "arbitrary"),
        name="swa_attention",
    )(sink.astype(F32), proj, proj, proj, proj, proj, proj, proj)


def _log_sigmoid(x):
    return jnp.minimum(x, 0.0) - jnp.log(1.0 + jnp.exp(-jnp.abs(x)))


def _split3_bf16(x):
    hi = x.astype(BF16)
    r = x - hi.astype(F32)
    mid = r.astype(BF16)
    lo = (r - mid.astype(F32)).astype(BF16)
    return hi, mid, lo


def _mlstm_kernel(gb_ref, q_ref, k_ref, v_ref, og_ref, gr_ref, cwq_ref, cwk_ref, cbq_ref, cbk_ref, ng_ref, o_ref,
                  qs, kst, vaug, hf, hb, rmat, *, seq, n_heads, hp):
    pair = pl.program_id(1)
    L, d, halo = MLSTM_CHUNK, HEAD_DIM, BF16_SUBLANES
    nc = seq // L
    pad = (CONV_WIDTH - 1) // 2
    w = hp * d

    tt = lax.broadcasted_iota(I32, (L, L), 0)
    ss = lax.broadcasted_iota(I32, (L, L), 1)
    for rev in (False, True):
        incl_t = (tt >= ss) if rev else (tt <= ss)
        rmat[int(rev), 0:L, :] = jnp.ones((L, L), BF16)
        rmat[int(rev), L:2 * L, :] = jnp.where(incl_t, -1.0, 0.0).astype(BF16)

    def conv_body(c, carry):
        base = pl.multiple_of(c * L, L)
        lo = pl.multiple_of(jnp.maximum(base - halo, 0), halo)
        hi = pl.multiple_of(jnp.minimum(base + L, seq - halo), halo)
        rows = lax.broadcasted_iota(I32, (L + 2 * halo, 1), 0) + (base - halo)

        def conv(src, w_ref, b_ref):
            x = jnp.concatenate([src[pl.ds(lo, halo), :], src[pl.ds(base, L), :], src[pl.ds(hi, halo), :]],
                                axis=0).astype(F32)
            x = jnp.where(rows >= 0, jnp.where(rows < seq, x, 0.0), 0.0)
            acc = jnp.zeros((L, w), F32) + b_ref[...]
            for j in range(CONV_WIDTH):
                start = halo + j - pad
                acc = acc + x[start:start + L] * w_ref[j:j + 1, :]
            return _silu(acc)

        qs[pl.ds(base, L), :] = conv(q_ref, cwq_ref, cbq_ref).astype(BF16)
        kc = conv(k_ref, cwk_ref, cbk_ref) * (d ** -0.5)
        for hh in range(hp):
            kst[hh * d:(hh + 1) * d, pl.ds(base, L)] = kc[:, hh * d:(hh + 1) * d].T.astype(BF16)
            vaug[pl.ds(base, L), 2 * hh * d:(2 * hh + 1) * d] = v_ref[pl.ds(base, L), hh * d:(hh + 1) * d]
            vaug[pl.ds(base, L), (2 * hh + 1) * d:(2 * hh + 2) * d] = jnp.ones((L, d), BF16)
        return carry

    lax.fori_loop(0, nc, conv_body, 0)

    def scan_chunk(c, carry, rev, hh):
        Caug, m = carry
        base = pl.multiple_of(c * L, L)
        head = pair * hp + hh
        gi, gf = (2, 3) if rev else (0, 1)
        q = qs[pl.ds(base, L), hh * d:(hh + 1) * d]
        kt = kst[hh * d:(hh + 1) * d, pl.ds(base, L)]
        va = vaug[pl.ds(base, L), 2 * hh * d:(2 * hh + 2) * d]
        i_row = gr_ref[hh, gi:gi + 1, pl.ds(base, L)] + gb_ref[gi * n_heads + head]
        f_row = _log_sigmoid(gr_ref[hh, gf:gf + 1, pl.ds(base, L)] + gb_ref[gf * n_heads + head])
        incl = (ss >= tt) if rev else (ss <= tt)
        dmat = jnp.zeros((L, L), F32)
        for part in _split3_bf16(f_row):
            fb = jnp.broadcast_to(part.astype(F32), (L, L))
            x = jnp.concatenate([jnp.where(incl, fb, 0.0), fb], axis=1).astype(BF16)
            dmat = dmat + jnp.dot(x, rmat[int(rev)], preferred_element_type=F32)
        first, last = (L - 1, 0) if rev else (0, L - 1)
        f_first = f_row[:, first:first + 1]
        b_col = dmat[:, first:first + 1] + f_first
        b_last = dmat[last:last + 1, first:first + 1] + f_first
        log_w = jnp.where(incl, dmat + i_row, -jnp.inf)
        log_inter = b_col + m
        m_t = jnp.maximum(log_inter, jnp.max(log_w, axis=-1, keepdims=True))
        w_intra = jnp.exp(log_w - m_t)
        w_inter = jnp.exp(log_inter - m_t)
        s = jnp.dot(q, kt, preferred_element_type=F32) * w_intra
        numden = (w_inter * jnp.dot(q, Caug.astype(BF16), preferred_element_type=F32)
                  + jnp.dot(s.astype(BF16), va, preferred_element_type=F32))
        inv = 1.0 / jnp.maximum(jnp.abs(numden[:, d:d + 1]), jnp.exp(-m_t))
        h_out = numden[:, 0:d] * inv
        log_ws = dmat[last:last + 1, :] + i_row
        m_new = jnp.maximum(b_last + m, jnp.max(log_ws, axis=-1, keepdims=True))
        decay = jnp.exp(b_last + m - m_new)
        kw = (kt.astype(F32) * jnp.exp(log_ws - m_new)).astype(BF16)
        Caug = decay * Caug + jnp.dot(kw, va, preferred_element_type=F32)
        return (Caug, m_new), h_out

    def scan_body(t, carries):
        out = []
        for hh in range(hp):
            for rev in (False, True):
                c = (nc - 1 - t) if rev else t
                carry, h_out = scan_chunk(c, carries[2 * hh + int(rev)], rev, hh)
                dst = hb if rev else hf
                dst[pl.ds(pl.multiple_of(c * L, L), L), hh * d:(hh + 1) * d] = h_out
                out.append(carry)
        return tuple(out)

    init = tuple((jnp.zeros((d, 2 * d), F32), jnp.zeros((1, 1), F32)) for _ in range(2 * hp))
    lax.fori_loop(0, nc, scan_body, init)

    slab = 4 * L

    def out_body(c, carry):
        base = pl.multiple_of(c * slab, slab)
        gate = jax.nn.sigmoid(og_ref[pl.ds(base, slab), :].astype(F32))
        for hh in range(hp):
            cols = slice(hh * d, (hh + 1) * d)
            h = hf[pl.ds(base, slab), cols] + hb[pl.ds(base, slab), cols]
            h = h * lax.rsqrt(jnp.mean(h * h, axis=-1, keepdims=True) + EPS) * ng_ref[:, cols]
            o_ref[pl.ds(base, slab), cols] = (h * gate[:, cols]).astype(o_ref.dtype)
        return carry

    lax.fori_loop(0, seq // slab, out_body, 0)


def _mlstm(proj, gates_row, gate_b, conv_w, conv_b, norm_g, batch, seq, n_heads, q_col, hp=2):
    T = proj.shape[0]
    d = HEAD_DIM
    width = n_heads * d
    w = hp * d
    c0 = q_col // w
    npair = n_heads // hp
    col = lambda off: (lambda b, h, *_: (b, c0 + off * npair + h))
    seq_blk = lambda off: pl.BlockSpec((seq, w), col(off))
    grid_spec = pltpu.PrefetchScalarGridSpec(
        num_scalar_prefetch=1,
        grid=(batch, npair),
        in_specs=[
            seq_blk(0), seq_blk(1), seq_blk(2), seq_blk(3),
            pl.BlockSpec((None, hp, 4, seq), lambda b, h, *_: (b, h, 0, 0)),
            pl.BlockSpec((CONV_WIDTH, w), lambda b, h, *_: (0, h)),
            pl.BlockSpec((CONV_WIDTH, w), lambda b, h, *_: (0, npair + h)),
            pl.BlockSpec((1, w), lambda b, h, *_: (0, h)),
            pl.BlockSpec((1, w), lambda b, h, *_: (0, npair + h)),
            pl.BlockSpec((1, w), lambda b, h, *_: (0, h)),
        ],
        out_specs=pl.BlockSpec((seq, w), lambda b, h, *_: (b, h)),
        scratch_shapes=[pltpu.VMEM((seq, w), BF16), pltpu.VMEM((w, seq), BF16), pltpu.VMEM((seq, 2 * w), BF16),
                        pltpu.VMEM((seq, w), F32), pltpu.VMEM((seq, w), F32),
                        pltpu.VMEM((2, 2 * MLSTM_CHUNK, MLSTM_CHUNK), BF16)],
    )
    return pl.pallas_call(
        functools.partial(_mlstm_kernel, seq=seq, n_heads=n_heads, hp=hp),
        grid_spec=grid_spec,
        out_shape=jax.ShapeDtypeStruct((T, width), BF16),
        compiler_params=_params("arbitrary", "arbitrary"),
        name="mlstm_bidir",
    )(gate_b.astype(F32), proj, proj, proj, proj, gates_row, conv_w, conv_w,
      conv_b.reshape(1, 2 * width), conv_b.reshape(1, 2 * width), norm_g.reshape(1, width))


def _diff_kernel(lam_ref, ng_ref, q_ref, k_ref, v_ref, o_ref, vt_sc, qt_sc, acc_sc, *, tq, tk, seq, lam_init):
    d = HEAD_DIM
    qs = DIFF_QUERY_STRIP

    @pl.when(pl.program_id(2) == 0)
    def _():
        for c in range(seq // tk):
            vt_sc[0:d, c * tk:(c + 1) * tk] = v_ref[c * tk:(c + 1) * tk, :].astype(F32).T.astype(BF16)
        vt_sc[d:, :] = jnp.ones((vt_sc.shape[0] - d, seq), BF16)

    qt = q_ref[...].astype(F32).T
    row = lax.broadcasted_iota(I32, (d, qs), 0)
    for s in range(tq // qs):
        qts = qt[:, s * qs:(s + 1) * qs]
        qt_sc[:, 2 * s * qs:(2 * s + 1) * qs] = jnp.where(row < d // 2, qts, 0.0).astype(BF16)
        qt_sc[:, (2 * s + 1) * qs:(2 * s + 2) * qs] = jnp.where(row >= d // 2, qts, 0.0).astype(BF16)

    lam = lam_ref[...]
    lam_full = (jnp.exp(jnp.sum(lam[0:1] * lam[1:2], axis=-1, keepdims=True))
                - jnp.exp(jnp.sum(lam[2:3] * lam[3:4], axis=-1, keepdims=True)) + lam_init)

    n_strips = tq // qs
    nk = seq // tk
    m = jnp.full((1, 2 * tq), -jnp.inf, F32)
    acc_sc[...] = jnp.zeros(acc_sc.shape, F32)
    scores = lambda j: jnp.dot(k_ref[j * tk:(j + 1) * tk, :], qt_sc[...], preferred_element_type=F32)
    st = scores(0)
    for j in range(nk):
        st_next = scores(j + 1) if j + 1 < nk else None
        m_new = jnp.maximum(m, jnp.max(st, axis=0, keepdims=True))
        p = jnp.exp2(st - m_new).astype(BF16)
        acc_sc[...] = jnp.exp2(m - m_new) * acc_sc[...] + jnp.dot(vt_sc[:, j * tk:(j + 1) * tk], p,
                                                                preferred_element_type=F32)
        m = m_new
        st = st_next
    for s in range(n_strips):
        acc = acc_sc[:, 2 * s * qs:(2 * s + 2) * qs]
        num = acc[0:d] / acc[d:d + 1]
        ot = num[:, 0:qs] - lam_full * num[:, qs:2 * qs]
        o = ot.T
        o = o * lax.rsqrt(jnp.mean(o * o, axis=-1, keepdims=True) + EPS) * ng_ref[...] * (1.0 - lam_init)
        o_ref[s * qs:(s + 1) * qs, :] = o.astype(o_ref.dtype)


def _diff_attention(proj, lam, norm_g, batch, seq, n_heads, q_col, layer_idx, tq=1024, tk=512):
    T = proj.shape[0]
    d = HEAD_DIM
    nq = seq // tq
    c0 = q_col // d
    lam_init = 0.8 - 0.6 * math.exp(-0.3 * layer_idx)
    return pl.pallas_call(
        functools.partial(_diff_kernel, tq=tq, tk=tk, seq=seq, lam_init=lam_init),
        grid=(batch, n_heads, nq),
        in_specs=[
            pl.BlockSpec(lam.shape, lambda b, h, i: (0, 0)),
            pl.BlockSpec((1, d), lambda b, h, i: (0, 0)),
            pl.BlockSpec((tq, d), lambda b, h, i: (b * nq + i, c0 + h)),
            pl.BlockSpec((seq, d), lambda b, h, i: (b, c0 + n_heads + h)),
            pl.BlockSpec((seq, d), lambda b, h, i: (b, c0 + 2 * n_heads + h)),
        ],
        out_specs=pl.BlockSpec((tq, d), lambda b, h, i: (b * nq + i, h)),
        out_shape=jax.ShapeDtypeStruct((T, n_heads * d), BF16),
        scratch_shapes=[pltpu.VMEM((d + BF16_SUBLANES, seq), BF16), pltpu.VMEM((d, 2 * tq), BF16),
                        pltpu.VMEM((d + BF16_SUBLANES, 2 * tq), F32)],
        compiler_params=_params("arbitrary", "arbitrary", "arbitrary"),
        name="diff_attention",
    )(lam.astype(F32), norm_g.reshape(1, d).astype(F32), proj, proj, proj)


def _outproj_kernel(a_ref, b_ref, c_ref, w_ref, x_ref, g_ref, o_ref):
    ka, kb = a_ref.shape[1], b_ref.shape[1]
    acc = jnp.dot(a_ref[...], w_ref[0:ka, :], preferred_element_type=F32)
    acc = acc + jnp.dot(b_ref[...], w_ref[ka:ka + kb, :], preferred_element_type=F32)
    acc = acc + jnp.dot(c_ref[...], w_ref[ka + kb:, :], preferred_element_type=F32)
    o_ref[...] = x_ref[...] + g_ref[0] * acc


def _out_projection(a, b, c, w, x, gate, seq, tm=1024, tn=512):
    T, D = x.shape
    B = gate.shape[0]
    per_b = seq // tm
    K = w.shape[0]
    return pl.pallas_call(
        _outproj_kernel,
        grid=(T // tm, D // tn),
        in_specs=[
            pl.BlockSpec((tm, a.shape[1]), lambda i, j: (i, 0)),
            pl.BlockSpec((tm, b.shape[1]), lambda i, j: (i, 0)),
            pl.BlockSpec((tm, c.shape[1]), lambda i, j: (i, 0)),
            pl.BlockSpec((K, tn), lambda i, j: (0, j)),
            pl.BlockSpec((tm, tn), lambda i, j: (i, j)),
            pl.BlockSpec((1, 1, tn), lambda i, j: (i // per_b, 0, j)),
        ],
        out_specs=pl.BlockSpec((tm, tn), lambda i, j: (i, j)),
        out_shape=jax.ShapeDtypeStruct((T, D), F32),
        compiler_params=_params("arbitrary", "arbitrary"),
        name="out_projection_residual",
    )(a, b, c, w, x, gate.reshape(B, 1, D))


def _shared_gu_kernel(h_ref, sg_ref, su_ref, o_ref):
    h = h_ref[...]
    g = jnp.dot(h, sg_ref[...], preferred_element_type=F32)
    u = jnp.dot(h, su_ref[...], preferred_element_type=F32)
    o_ref[...] = (_silu(g) * u).astype(o_ref.dtype)


def _shared_gate_up(h, sg, su, tm=512):
    T, D = h.shape
    Fs = sg.shape[1]
    return pl.pallas_call(
        _shared_gu_kernel,
        grid=(T // tm,),
        in_specs=[pl.BlockSpec((tm, D), lambda i: (i, 0)),
                  pl.BlockSpec((D, Fs), lambda i: (0, 0)),
                  pl.BlockSpec((D, Fs), lambda i: (0, 0))],
        out_specs=pl.BlockSpec((tm, Fs), lambda i: (i, 0)),
        out_shape=jax.ShapeDtypeStruct((T, Fs), BF16),
        compiler_params=_params("arbitrary"),
        name="shared_gate_up",
    )(h, sg, su)


def _row_gather_start(idx_ref, n_rows, src_hbm, dst, sem):
    def body(r, carry):
        t = idx_ref[0, 0, r]
        pltpu.make_async_copy(src_hbm.at[pl.ds(t, 1), :], dst.at[pl.ds(r, 1), :], sem).start()
        return carry
    lax.fori_loop(0, n_rows, body, 0, unroll=8)


def _expert_kernel(be_ref, nact_ref, idx0_ref, idxn_ref, h_hbm, wg_ref, wu_ref, wd_ref, y_ref, xbuf, sem, *, tm):
    i = pl.program_id(0)
    last = pl.num_programs(0) - 1
    nact = nact_ref[0]

    def gather_wait(slot):
        pltpu.make_async_copy(h_hbm.at[pl.ds(0, tm), :], xbuf.at[slot], sem.at[slot]).wait()

    @pl.when(i == 0)
    def _():
        _row_gather_start(idx0_ref, tm, h_hbm, xbuf.at[0], sem.at[0])

    def compute(slot):
        def start_rows(lo, hi):
            for r in range(lo, hi):
                t = idxn_ref[0, 0, r]
                pltpu.make_async_copy(h_hbm.at[pl.ds(t, 1), :], xbuf.at[1 - slot, pl.ds(r, 1), :],
                                      sem.at[1 - slot]).start()

        gather_wait(slot)
        x = xbuf[slot].astype(BF16)
        g = jnp.dot(x, wg_ref[0].astype(BF16), preferred_element_type=F32)
        start_rows(0, tm // 2)
        u = jnp.dot(x, wu_ref[0].astype(BF16), preferred_element_type=F32)
        start_rows(tm // 2, tm)
        hid = (_silu(g) * u).astype(BF16)
        y_ref[...] = jnp.dot(hid, wd_ref[0].astype(BF16), preferred_element_type=F32)

    for slot in (0, 1):
        pl.when(jnp.logical_and(i < nact, i % 2 == slot))(functools.partial(compute, slot))

    @pl.when(i == nact)
    def _():
        gather_wait(i % 2)

    @pl.when(jnp.logical_and(i == last, nact == last + 1))
    def _():
        gather_wait((i + 1) % 2)

    @pl.when(i >= nact)
    def _():
        y_ref[...] = jnp.zeros(y_ref.shape, y_ref.dtype)


def _routed_experts(h32, row_t, blk_e, nact, wg, wu, wd, layer, tm):
    T, D = h32.shape
    _, E, _, Fe = wg.shape
    nblk = row_t.shape[0] // tm
    idx = row_t.reshape(nblk, 1, tm)
    grid_spec = pltpu.PrefetchScalarGridSpec(
        num_scalar_prefetch=2,
        grid=(nblk,),
        in_specs=[
            pl.BlockSpec((1, 1, tm), lambda i, be, na: (0, 0, 0), memory_space=pltpu.SMEM),
            pl.BlockSpec((1, 1, tm), lambda i, be, na: (jnp.minimum(i + 1, nblk - 1), 0, 0),
                         memory_space=pltpu.SMEM),
            pl.BlockSpec(memory_space=pl.ANY),
            pl.BlockSpec((None, 1, D, Fe), lambda i, be, na: (layer, be[i], 0, 0), pipeline_mode=pl.Buffered(1)),
            pl.BlockSpec((None, 1, D, Fe), lambda i, be, na: (layer, be[i], 0, 0), pipeline_mode=pl.Buffered(1)),
            pl.BlockSpec((None, 1, Fe, D), lambda i, be, na: (layer, be[i], 0, 0), pipeline_mode=pl.Buffered(1)),
        ],
        out_specs=pl.BlockSpec((tm, D), lambda i, be, na: (i, 0)),
        scratch_shapes=[pltpu.VMEM((2, tm, D), F32), pltpu.SemaphoreType.DMA((2,))],
    )
    return pl.pallas_call(
        functools.partial(_expert_kernel, tm=tm),
        grid_spec=grid_spec,
        out_shape=jax.ShapeDtypeStruct((nblk * tm, D), F32),
        compiler_params=_params("arbitrary"),
        name="routed_experts",
    )(blk_e, nact, idx, idx, h32, wg, wu, wd)


def _combine_kernel(pos0_ref, posn_ref, w_ref, hs_ref, sd_ref, x_ref, g_ref, y_hbm, ng_ref, *refs, tc, top_k,
                    next_layer):
    if next_layer:
        sc_ref, sh_ref, ws_ref, o_ref, h_ref, st_ref, buf, sem = refs
    else:
        o_ref, buf, sem = refs
    i = pl.program_id(0)
    n = pl.num_programs(0)
    slot = i % 2

    def start(pos_ref, s):
        for k in range(top_k):
            def body(r, carry):
                p = pos_ref[0, 0, k * tc + r]
                pltpu.make_async_copy(y_hbm.at[pl.ds(p, 1), :], buf.at[s, k, pl.ds(r, 1), :], sem.at[s]).start()
                return carry
            lax.fori_loop(0, tc, body, 0, unroll=8)

    @pl.when(i == 0)
    def _():
        start(pos0_ref, 0)

    @pl.when(i + 1 < n)
    def _():
        start(posn_ref, 1 - slot)

    for k in range(top_k):
        pltpu.make_async_copy(y_hbm.at[pl.ds(0, tc), :], buf.at[slot, k], sem.at[slot]).wait()
    w = w_ref[...]
    routed = buf[slot, 0] * w[:, 0:1]
    for k in range(1, top_k):
        routed = routed + buf[slot, k] * w[:, k:k + 1]
    shared = jnp.dot(hs_ref[...], sd_ref[...], preferred_element_type=F32)
    xn = x_ref[...] + g_ref[0] * (routed + shared)
    normed = xn * lax.rsqrt(jnp.mean(xn * xn, axis=-1, keepdims=True) + EPS) * ng_ref[...]
    if next_layer:
        o_ref[...] = xn
        h = normed * (1.0 + sc_ref[0]) + sh_ref[0]
        h_ref[...] = h.astype(BF16)
        st_ref[...] = jnp.dot(h, ws_ref[...], preferred_element_type=F32, precision=lax.Precision.HIGHEST).T
    else:
        o_ref[...] = normed


def _moe_combine(y, pos, w, hs, sd, x, gate, seq, norm_g, next_mod=None, w_small=None, tc=128):
    T, D = x.shape
    B = gate.shape[0]
    K = pos.shape[0]
    Fs = hs.shape[1]
    nt = T // tc
    per_b = seq // tc
    pos_tiles = pos.reshape(K, nt, tc).transpose(1, 0, 2).reshape(nt, 1, K * tc)
    w = w.T
    row = lambda i: (i, 0)
    per_batch = lambda i: (i // per_b, 0, 0)
    in_specs = [
        pl.BlockSpec((1, 1, K * tc), lambda i: (0, 0, 0), memory_space=pltpu.SMEM),
        pl.BlockSpec((1, 1, K * tc), lambda i: (jnp.minimum(i + 1, nt - 1), 0, 0), memory_space=pltpu.SMEM),
        pl.BlockSpec((tc, K), row),
        pl.BlockSpec((tc, Fs), row),
        pl.BlockSpec((Fs, D), lambda i: (0, 0)),
        pl.BlockSpec((tc, D), row),
        pl.BlockSpec((1, 1, D), per_batch),
        pl.BlockSpec(memory_space=pl.ANY),
        pl.BlockSpec((1, D), lambda i: (0, 0)),
    ]
    args = [pos_tiles, pos_tiles, w, hs, sd, x, gate.reshape(B, 1, D), y, norm_g.reshape(1, D)]
    out_specs = [pl.BlockSpec((tc, D), row)]
    out_shape = [jax.ShapeDtypeStruct((T, D), F32)]
    if next_mod is not None:
        ns = w_small.shape[1]
        in_specs += [pl.BlockSpec((1, 1, D), per_batch), pl.BlockSpec((1, 1, D), per_batch),
                     pl.BlockSpec((D, ns), lambda i: (0, 0))]
        args += [next_mod[0].reshape(B, 1, D), next_mod[1].reshape(B, 1, D), w_small]
        out_specs += [pl.BlockSpec((tc, D), row), pl.BlockSpec((ns, tc), lambda i: (0, i))]
        out_shape += [jax.ShapeDtypeStruct((T, D), BF16), jax.ShapeDtypeStruct((ns, T), F32)]
    out = pl.pallas_call(
        functools.partial(_combine_kernel, tc=tc, top_k=K, next_layer=next_mod is not None),
        grid=(nt,),
        in_specs=in_specs,
        out_specs=out_specs,
        out_shape=out_shape,
        scratch_shapes=[pltpu.VMEM((2, K, tc, D), F32), pltpu.SemaphoreType.DMA((2,))],
        compiler_params=_params("arbitrary"),
        name="moe_combine_residual",
    )(*args)
    return out if next_mod is not None else out[0]


def _route_kernel(lt_ref, b_ref, u_ref, eid_ref, rnk_ref, w_ref, cnt_ref, carry_sc):
    G, P, K = N_GROUPS, N_EXPERTS // N_GROUPS, TOP_K
    E = G * P
    tr = lt_ref.shape[1]

    @pl.when(pl.program_id(0) == 0)
    def _():
        carry_sc[...] = jnp.zeros(carry_sc.shape, F32)

    one = lambda cond: jnp.where(cond, 1.0, 0.0)
    gidx = lax.broadcasted_iota(I32, (G, tr), 0)
    rows = lambda a, g: jnp.broadcast_to(a[g:g + 1, :], (G, tr))
    score = [jax.nn.sigmoid(lt_ref[j * G:(j + 1) * G, :]) for j in range(P)]
    biased = [score[j] + b_ref[j * G:(j + 1) * G, 0:1] for j in range(P)]

    top1, top2 = biased[0], jnp.full((G, tr), -jnp.inf, F32)
    for j in range(1, P):
        top2 = jnp.maximum(top2, jnp.minimum(top1, biased[j]))
        top1 = jnp.maximum(top1, biased[j])
    gscore = top1 + top2

    beaten = jnp.zeros((G, tr), F32)
    for c in range(G):
        vc = rows(gscore, c)
        beaten = beaten + one(vc > gscore) + jnp.where(gidx > c, one(vc == gscore), 0.0)
    keep = beaten < TOPK_GROUPS
    masked = [jnp.where(keep, biased[j], -jnp.inf) for j in range(P)]

    rank = [jnp.zeros((G, tr), F32) for _ in range(P)]
    for jc in range(P):
        for gc in range(G):
            vc = rows(masked[jc], gc)
            for j in range(P):
                lower_id = (gidx >= gc) if j > jc else (gidx > gc)
                rank[j] = rank[j] + one(vc > masked[j]) + jnp.where(lower_id, one(vc == masked[j]), 0.0)
    sel = [jnp.where(keep, one(rank[j] < K), 0.0) for j in range(P)]

    picked = sel[0] * score[0]
    for j in range(1, P):
        picked = picked + sel[j] * score[j]
    total = jnp.sum(picked, axis=0, keepdims=True)
    weight = [sel[j] * score[j] / total * ROUTED_SCALE for j in range(P)]

    sel_all = jnp.concatenate(sel, axis=0).astype(BF16)
    earlier = jnp.dot(sel_all, u_ref[...], preferred_element_type=F32)
    carry = carry_sc[...]
    place = [earlier[j * G:(j + 1) * G, :] + carry[j * G:(j + 1) * G, 0:1] for j in range(P)]
    carry_sc[0:E, :] = carry[0:E, :] + jnp.dot(sel_all, jnp.ones((tr, LANES), BF16), preferred_element_type=F32)
    cnt_ref[...] = carry_sc[...]

    eid = [(gidx * P + j).astype(F32) for j in range(P)]
    pick = lambda vals, hit: jnp.sum(sum(hit[j] * vals[j] for j in range(P)), axis=0, keepdims=True)
    out_e, out_r, out_w = [], [], []
    for k in range(K):
        hit = [sel[j] * one(rank[j] == k) for j in range(P)]
        out_e.append(pick(eid, hit))
        out_r.append(pick(place, hit))
        out_w.append(pick(weight, hit))
    fill = [jnp.zeros((eid_ref.shape[0] - K, tr), F32)]
    eid_ref[...] = jnp.concatenate(out_e + fill, axis=0).astype(I32)
    rnk_ref[...] = jnp.concatenate(out_r + fill, axis=0).astype(I32)
    w_ref[...] = jnp.concatenate(out_w + fill, axis=0)


def _route(logits_t, bias_col, tr=512):
    T = logits_t.shape[1]
    slots = 8
    strictly_earlier = jnp.triu(jnp.ones((tr, tr), BF16), 1)
    tile = lambda i: (0, i)
    eid, place, w, cnt = pl.pallas_call(
        _route_kernel,
        grid=(T // tr,),
        in_specs=[pl.BlockSpec((LANES, tr), tile),
                  pl.BlockSpec((LANES, 1), lambda i: (0, 0)),
                  pl.BlockSpec((tr, tr), lambda i: (0, 0))],
        out_specs=[pl.BlockSpec((slots, tr), tile), pl.BlockSpec((slots, tr), tile), pl.BlockSpec((slots, tr), tile),
                   pl.BlockSpec((LANES, LANES), lambda i: (0, 0))],
        out_shape=[jax.ShapeDtypeStruct((slots, T), I32), jax.ShapeDtypeStruct((slots, T), I32),
                   jax.ShapeDtypeStruct((slots, T), F32), jax.ShapeDtypeStruct((LANES, LANES), F32)],
        scratch_shapes=[pltpu.VMEM((LANES, LANES), F32)],
        compiler_params=_params("arbitrary"),
        name="moe_routing",
    )(logits_t, bias_col, strictly_earlier)
    return eid[:TOP_K], place[:TOP_K], w[:TOP_K], cnt[:N_EXPERTS, 0].astype(I32)


def _router_row_order(a):
    G, P = N_GROUPS, N_EXPERTS // N_GROUPS
    return a.reshape(a.shape[:-1] + (G, P)).swapaxes(-1, -2).reshape(a.shape[:-1] + (G * P,))


def _dispatch_plan(eid, place, counts_rows, tm):
    K, T = eid.shape
    E, G, P = N_EXPERTS, N_GROUPS, N_EXPERTS // N_GROUPS
    counts = counts_rows.reshape(P, G).T.reshape(E)
    padded = (counts + tm - 1) // tm * tm
    pad_end = jnp.cumsum(padded)
    pad_start = pad_end - padded
    ids = jnp.arange(E, dtype=I32)
    dest = place + jnp.sum(jnp.where(eid[..., None] == ids, pad_start, 0), axis=-1)
    nblk = K * T // tm + E
    tok = jnp.broadcast_to(jnp.arange(T, dtype=I32)[None, :], (K, T))
    row_t = jnp.zeros((nblk * tm,), I32).at[dest.reshape(-1)].set(tok.reshape(-1), unique_indices=True)
    nact = (pad_end[-1] // tm).astype(I32)
    blk = jnp.minimum(jnp.arange(nblk, dtype=I32), nact - 1)
    blk_e = jnp.sum((pad_end[None, :] <= (blk * tm)[:, None]).astype(I32), axis=1)
    return row_t, blk_e, nact.reshape(1), dest


def kernel(x, c, ada_w, ada_b, ada_table, norm1_g, w_in, swa_sink, mlstm_conv_w, mlstm_conv_b, mlstm_gate_b,
           mlstm_norm_g, diff_lambda, diff_norm_g, w_out, norm2_g, router_w, router_b, exp_gate, exp_up,
           exp_down, sh_gate, sh_up, sh_down, final_g):
    B, S, D = x.shape
    depth = w_in.shape[0]
    T = B * S
    d = HEAD_DIM
    n_swa_q = swa_sink.shape[1]
    n_swa_kv = n_swa_q // SWA_GROUP
    n_ml = mlstm_gate_b.shape[1] // 4
    n_diff = (D // d) - n_swa_q - n_ml
    swa_q_w, swa_kv_w, ml_w, diff_w = n_swa_q * d, n_swa_kv * d, n_ml * d, n_diff * d
    gate_w = 4 * n_ml
    gate_off = swa_q_w + 2 * swa_kv_w + 4 * ml_w
    k_a, v_a = swa_q_w, swa_q_w + swa_kv_w
    q_m = swa_q_w + 2 * swa_kv_w
    q_c = gate_off
    tn = 512
    diff_q_scale = (d // 2) ** -0.5 * math.log2(math.e)
    tiles = ((swa_q_w + swa_kv_w) // tn, q_c // tn, (q_c + diff_w) // tn, (q_c + 2 * diff_w) // tn, diff_q_scale)

    cos_a, sin_a = _rope_tables(S, d // 2, tn)
    cos_c, sin_c = _rope_tables(S, d // 4, tn)
    rope_tabs = (cos_a, sin_a, cos_c, sin_c)

    mod = _modulation(c, ada_w, ada_b, ada_table)
    gate_cols = lambda l: jnp.pad(w_in[l, :, gate_off:gate_off + gate_w], ((0, 0), (0, LANES - gate_w)))
    xt = x.reshape(T, D)
    h, gates = _norm_mod(xt, norm1_g[0], mod[0, :, 1], mod[0, :, 0], gate_cols(0), S, emit_f32=False)
    for l in range(depth):
        sh_a, sc_a, g_a, sh_f, sc_f, g_f = [mod[l, :, i] for i in range(ADA_CHUNKS)]
        w_main = jnp.concatenate([w_in[l, :, :gate_off], w_in[l, :, gate_off + gate_w:]], axis=1).astype(BF16)
        proj = _in_projection(h, w_main, rope_tabs, S, tiles, tn=tn)
        out_a = _swa_attention(proj, swa_sink[l], B, S, n_swa_kv, 0, k_a, v_a)
        g4 = gates[:gate_w].reshape(4, n_ml, B, S)
        out_b = _mlstm(proj, g4.transpose(2, 1, 0, 3), mlstm_gate_b[l],
                       mlstm_conv_w[l], mlstm_conv_b[l], mlstm_norm_g[l], B, S, n_ml, q_m)
        out_c = _diff_attention(proj, diff_lambda[l], diff_norm_g[l], B, S, n_diff, q_c, l)
        xt = _out_projection(out_a, out_b, out_c, w_out[l].astype(BF16), xt, g_a, S)

        w_router = jnp.pad(_router_row_order(router_w[l]), ((0, 0), (0, LANES - N_EXPERTS)))
        b_router = jnp.pad(_router_row_order(router_b[l].astype(F32)), (0, LANES - N_EXPERTS)).reshape(LANES, 1)
        h, h32, logits_t = _norm_mod(xt, norm2_g[l], sc_f, sh_f, w_router, S, emit_f32=True)
        eid, place, wts, counts = _route(logits_t, b_router)
        tm = 256
        row_t, blk_e, nact, pos = _dispatch_plan(eid, place, counts, tm)
        y = _routed_experts(h32, row_t, blk_e, nact, exp_gate, exp_up, exp_down, l, tm)
        hs = _shared_gate_up(h, sh_gate[l].astype(BF16), sh_up[l].astype(BF16))
        sd = sh_down[l].astype(BF16)
        if l + 1 < depth:
            xt, h, gates = _moe_combine(y, pos, wts, hs, sd, xt, g_f, S, norm1_g[l + 1],
                                        next_mod=(mod[l + 1, :, 1], mod[l + 1, :, 0]), w_small=gate_cols(l + 1))
        else:
            xt = _moe_combine(y, pos, wts, hs, sd, xt, g_f, S, final_g)
    return xt.reshape(B, S, D)
```

```python
import functools
import math

import jax
import jax.numpy as jnp
from jax import lax
from jax.experimental import pallas as pl
from jax.experimental.pallas import tpu as pltpu

F32 = jnp.float32
BF16 = jnp.bfloat16
I32 = jnp.int32

HEAD_DIM = 128
SWA_GROUP = 3
WINDOW = 128
MLSTM_CHUNK = 128
CONV_WIDTH = 5
ROPE_THETA = 10000.0
N_EXPERTS = 48
TOP_K = 6
N_GROUPS = 8
TOPK_GROUPS = 4
ROUTED_SCALE = 2.5
ADA_CHUNKS = 6
EPS = 1e-6

LANES = 128
BF16_SUBLANES = 16
MXU_WIDTH_V7X = 256
DIFF_QUERY_STRIP = MXU_WIDTH_V7X // 2
VMEM_LIMIT_V7X = 56 * 1024 * 1024

NT_DIMS = (((1,), (1,)), ((), ()))


def _params(*sem):
    return pltpu.CompilerParams(dimension_semantics=sem, vmem_limit_bytes=VMEM_LIMIT_V7X)


def _silu(x):
    return x * jax.nn.sigmoid(x)


def _mod_kernel(c_ref, w_ref, b_ref, tab_ref, o_ref):
    a = _silu(c_ref[...])
    cond = jnp.dot(a.astype(BF16), w_ref[...].astype(BF16), preferred_element_type=F32) + b_ref[...]
    for l in range(tab_ref.shape[0]):
        o_ref[l] = cond + tab_ref[l]


def _modulation(c, ada_w, ada_b, ada_table, tn=512):
    B, D = c.shape
    depth = ada_table.shape[0]
    N = ada_w.shape[1]
    rows = 8
    c_pad = jnp.pad(c, ((0, rows - B), (0, 0)))
    out = pl.pallas_call(
        _mod_kernel,
        grid=(N // tn,),
        in_specs=[
            pl.BlockSpec((rows, D), lambda j: (0, 0)),
            pl.BlockSpec((D, tn), lambda j: (0, j)),
            pl.BlockSpec((1, tn), lambda j: (0, j)),
            pl.BlockSpec((depth, 1, tn), lambda j: (0, 0, j)),
        ],
        out_specs=pl.BlockSpec((depth, rows, tn), lambda j: (0, 0, j)),
        out_shape=jax.ShapeDtypeStruct((depth, rows, N), F32),
        compiler_params=_params("arbitrary"),
        name="adaln_mod",
    )(c_pad, ada_w, ada_b.reshape(1, N), ada_table.reshape(depth, 1, N))
    return out[:, :B].reshape(depth, B, ADA_CHUNKS, D)


def _norm_kernel(x_ref, g_ref, sc_ref, sh_ref, ws_ref, *out_refs, emit_f32):
    x = x_ref[...]
    ms = jnp.mean(x * x, axis=-1, keepdims=True)
    h = x * lax.rsqrt(ms + EPS) * g_ref[...]
    h = h * (1.0 + sc_ref[0]) + sh_ref[0]
    out_refs[0][...] = h.astype(BF16)
    if emit_f32:
        out_refs[1][...] = h
    out_refs[-1][...] = jnp.dot(h, ws_ref[...], preferred_element_type=F32,
                                precision=lax.Precision.HIGHEST).T


def _norm_mod(x, g, scale, shift, w_small, seq, emit_f32, tr=256):
    T, D = x.shape
    B = scale.shape[0]
    ns = w_small.shape[1]
    per_b = seq // tr
    row = lambda i: (i, 0)
    out_shape = [jax.ShapeDtypeStruct((T, D), BF16)]
    out_specs = [pl.BlockSpec((tr, D), row)]
    if emit_f32:
        out_shape.append(jax.ShapeDtypeStruct((T, D), F32))
        out_specs.append(pl.BlockSpec((tr, D), row))
    out_shape.append(jax.ShapeDtypeStruct((ns, T), F32))
    out_specs.append(pl.BlockSpec((ns, tr), lambda i: (0, i)))
    return pl.pallas_call(
        functools.partial(_norm_kernel, emit_f32=emit_f32),
        grid=(T // tr,),
        in_specs=[
            pl.BlockSpec((tr, D), row),
            pl.BlockSpec((1, D), lambda i: (0, 0)),
            pl.BlockSpec((1, 1, D), lambda i: (i // per_b, 0, 0)),
            pl.BlockSpec((1, 1, D), lambda i: (i // per_b, 0, 0)),
            pl.BlockSpec((D, ns), lambda i: (0, 0)),
        ],
        out_specs=out_specs,
        out_shape=out_shape,
        compiler_params=_params("arbitrary"),
        name="rmsnorm_adaln",
    )(x, g.reshape(1, D), scale.reshape(B, 1, D), shift.reshape(B, 1, D), w_small)


def _rope_tile(acc, cos, sin, half):
    n = acc.shape[1]
    lane = lax.broadcasted_iota(I32, acc.shape, 1)
    from_right = pltpu.roll(acc, n - half, 1)
    from_left = pltpu.roll(acc, half, 1)
    rot = jnp.where((lane & (2 * half - 1)) < half, from_right, from_left)
    return acc * cos + rot * sin


def _inproj_kernel(a_ref, w_ref, cos_a, sin_a, cos_c, sin_c, o_ref, *, tiles):
    n_aqk, c_q0, c_k0, c_v0, q_scale = tiles
    j = pl.program_id(1)
    acc = jnp.dot(a_ref[...], w_ref[...], preferred_element_type=F32)

    @pl.when(j < n_aqk)
    def _():
        o_ref[...] = _rope_tile(acc, cos_a[...], sin_a[...], HEAD_DIM // 2).astype(o_ref.dtype)

    @pl.when(jnp.logical_and(j >= c_q0, j < c_k0))
    def _():
        o_ref[...] = (_rope_tile(acc, cos_c[...], sin_c[...], HEAD_DIM // 4) * q_scale).astype(o_ref.dtype)

    @pl.when(jnp.logical_and(j >= c_k0, j < c_v0))
    def _():
        o_ref[...] = _rope_tile(acc, cos_c[...], sin_c[...], HEAD_DIM // 4).astype(o_ref.dtype)

    @pl.when(jnp.logical_or(jnp.logical_and(j >= n_aqk, j < c_q0), j >= c_v0))
    def _():
        o_ref[...] = acc.astype(o_ref.dtype)


def _in_projection(h, w, rope_tabs, seq, tiles, tm=1024, tn=512):
    T, D = h.shape
    N = w.shape[1]
    per_b = seq // tm
    tab = pl.BlockSpec((tm, tn), lambda i, j: (i % per_b, 0))
    return pl.pallas_call(
        functools.partial(_inproj_kernel, tiles=tiles),
        grid=(T // tm, N // tn),
        in_specs=[
            pl.BlockSpec((tm, D), lambda i, j: (i, 0)),
            pl.BlockSpec((D, tn), lambda i, j: (0, j)),
            tab, tab, tab, tab,
        ],
        out_specs=pl.BlockSpec((tm, tn), lambda i, j: (i, j)),
        out_shape=jax.ShapeDtypeStruct((T, N), BF16),
        compiler_params=_params("arbitrary", "arbitrary"),
        name="in_projection_rope",
    )(h, w, *rope_tabs)


def _rope_tables(seq, half, width):
    inv = jnp.power(ROPE_THETA, -jnp.arange(half, dtype=F32) / half)
    ang = jnp.arange(seq, dtype=F32)[:, None] * inv[None, :]
    cos, sin = jnp.cos(ang), jnp.sin(ang)
    reps = width // (2 * half)
    return (jnp.tile(jnp.concatenate([cos, cos], axis=1), (1, reps)),
            jnp.tile(jnp.concatenate([-sin, sin], axis=1), (1, reps)))


def _swa_kernel(sink_ref, q_ref, kp_ref, km_ref, kn_ref, vp_ref, vm_ref, vn_ref, o_ref, *, tq, seq):
    hk = pl.program_id(1)
    i = pl.program_id(2)
    W, G, d = WINDOW, SWA_GROUP, HEAD_DIM
    kcat = jnp.concatenate([kp_ref[...], km_ref[...], kn_ref[...]], axis=0)
    vcat = jnp.concatenate([vp_ref[...], vm_ref[...], vn_ref[...]], axis=0)
    qi = lax.broadcasted_iota(I32, (G * W, 3 * W), 0) & (W - 1)
    kj = lax.broadcasted_iota(I32, (G * W, 3 * W), 1)
    scale = d ** -0.5
    for r in range(tq // W):
        kw = kcat[r * W:(r + 3) * W]
        vw = vcat[r * W:(r + 3) * W]
        qs = jnp.concatenate([q_ref[r * W:(r + 1) * W, g * d:(g + 1) * d] for g in range(G)], axis=0)
        s = lax.dot_general(qs, kw, NT_DIMS, preferred_element_type=F32) * scale
        first = i * tq + (r - 1) * W
        lo = jnp.maximum(qi, -first)
        hi = jnp.minimum(qi + 2 * W, seq - 1 - first)
        s = jnp.where(kj >= lo, jnp.where(kj <= hi, s, -jnp.inf), -jnp.inf)
        sk = jnp.concatenate([jnp.full((W, 1), sink_ref[hk * G + g], F32) for g in range(G)], axis=0)
        mx = jnp.maximum(jnp.max(s, axis=-1, keepdims=True), sk)
        p = jnp.exp(s - mx)
        denom = jnp.sum(p, axis=-1, keepdims=True) + jnp.exp(sk - mx)
        o = jnp.dot(p.astype(BF16), vw, preferred_element_type=F32) / denom
        for g in range(G):
            o_ref[r * W:(r + 1) * W, g * d:(g + 1) * d] = o[g * W:(g + 1) * W].astype(o_ref.dtype)


def _swa_attention(proj, sink, batch, seq, n_kv, q_col, k_col, v_col, tq=1024):
    T = proj.shape[0]
    W, G, d = WINDOW, SWA_GROUP, HEAD_DIM
    nq = seq // tq
    wpt = tq // W
    nw = seq // W
    qw = G * d
    kc, vc = k_col // d, v_col // d

    def prev_map(col):
        return lambda b, h, i, *_: (b * nw + jnp.maximum(i * wpt - 1, 0), col + h)

    def next_map(col):
        return lambda b, h, i, *_: (b * nw + jnp.minimum((i + 1) * wpt, nw - 1), col + h)

    def main_map(col):
        return lambda b, h, i, *_: (b * nq + i, col + h)

    grid_spec = pltpu.PrefetchScalarGridSpec(
        num_scalar_prefetch=1,
        grid=(batch, n_kv, nq),
        in_specs=[
            pl.BlockSpec((tq, qw), lambda b, h, i, *_: (b * nq + i, q_col // qw + h)),
            pl.BlockSpec((W, d), prev_map(kc)),
            pl.BlockSpec((tq, d), main_map(kc)),
            pl.BlockSpec((W, d), next_map(kc)),
            pl.BlockSpec((W, d), prev_map(vc)),
            pl.BlockSpec((tq, d), main_map(vc)),
            pl.BlockSpec((W, d), next_map(vc)),
        ],
        out_specs=pl.BlockSpec((tq, qw), lambda b, h, i, *_: (b * nq + i, h)),
    )
    return pl.pallas_call(
        functools.partial(_swa_kernel, tq=tq, seq=seq),
        grid_spec=grid_spec,
        out_shape=jax.ShapeDtypeStruct((T, n_kv * qw), BF16),
        compiler_params=_params("arbitrary", "arbitrary", "arbitrary"),
        name="swa_attention",
    )(sink.astype(F32), proj, proj, proj, proj, proj, proj, proj)


def _log_sigmoid(x):
    return jnp.minimum(x, 0.0) - jnp.log(1.0 + jnp.exp(-jnp.abs(x)))


def _split3_bf16(x):
    hi = x.astype(BF16)
    r = x - hi.astype(F32)
    mid = r.astype(BF16)
    lo = (r - mid.astype(F32)).astype(BF16)
    return hi, mid, lo


def _mlstm_kernel(gb_ref, q_ref, k_ref, v_ref, og_ref, gr_ref, cwq_ref, cwk_ref, cbq_ref, cbk_ref, ng_ref, o_ref,
                  qs, kst, vaug, hf, hb, rmat, *, seq, n_heads, hp):
    pair = pl.program_id(1)
    L, d, halo = MLSTM_CHUNK, HEAD_DIM, BF16_SUBLANES
    nc = seq // L
    pad = (CONV_WIDTH - 1) // 2
    w = hp * d

    tt = lax.broadcasted_iota(I32, (L, L), 0)
    ss = lax.broadcasted_iota(I32, (L, L), 1)
    for rev in (False, True):
        incl_t = (tt >= ss) if rev else (tt <= ss)
        rmat[int(rev), 0:L, :] = jnp.ones((L, L), BF16)
        rmat[int(rev), L:2 * L, :] = jnp.where(incl_t, -1.0, 0.0).astype(BF16)

    def conv_body(c, carry):
        base = pl.multiple_of(c * L, L)
        lo = pl.multiple_of(jnp.maximum(base - halo, 0), halo)
        hi = pl.multiple_of(jnp.minimum(base + L, seq - halo), halo)
        rows = lax.broadcasted_iota(I32, (L + 2 * halo, 1), 0) + (base - halo)

        def conv(src, w_ref, b_ref):
            x = jnp.concatenate([src[pl.ds(lo, halo), :], src[pl.ds(base, L), :], src[pl.ds(hi, halo), :]],
                                axis=0).astype(F32)
            x = jnp.where(rows >= 0, jnp.where(rows < seq, x, 0.0), 0.0)
            acc = jnp.zeros((L, w), F32) + b_ref[...]
            for j in range(CONV_WIDTH):
                start = halo + j - pad
                acc = acc + x[start:start + L] * w_ref[j:j + 1, :]
            return _silu(acc)

        qs[pl.ds(base, L), :] = conv(q_ref, cwq_ref, cbq_ref).astype(BF16)
        kc = conv(k_ref, cwk_ref, cbk_ref) * (d ** -0.5)
        for hh in range(hp):
            kst[hh * d:(hh + 1) * d, pl.ds(base, L)] = kc[:, hh * d:(hh + 1) * d].T.astype(BF16)
            vaug[pl.ds(base, L), 2 * hh * d:(2 * hh + 1) * d] = v_ref[pl.ds(base, L), hh * d:(hh + 1) * d]
            vaug[pl.ds(base, L), (2 * hh + 1) * d:(2 * hh + 2) * d] = jnp.ones((L, d), BF16)
        return carry

    lax.fori_loop(0, nc, conv_body, 0)

    def scan_chunk(c, carry, rev, hh):
        Caug, m = carry
        base = pl.multiple_of(c * L, L)
        head = pair * hp + hh
        gi, gf = (2, 3) if rev else (0, 1)
        q = qs[pl.ds(base, L), hh * d:(hh + 1) * d]
        kt = kst[hh * d:(hh + 1) * d, pl.ds(base, L)]
        va = vaug[pl.ds(base, L), 2 * hh * d:(2 * hh + 2) * d]
        i_row = gr_ref[hh, gi:gi + 1, pl.ds(base, L)] + gb_ref[gi * n_heads + head]
        f_row = _log_sigmoid(gr_ref[hh, gf:gf + 1, pl.ds(base, L)] + gb_ref[gf * n_heads + head])
        incl = (ss >= tt) if rev else (ss <= tt)
        dmat = jnp.zeros((L, L), F32)
        for part in _split3_bf16(f_row):
            fb = jnp.broadcast_to(part.astype(F32), (L, L))
            x = jnp.concatenate([jnp.where(incl, fb, 0.0), fb], axis=1).astype(BF16)
            dmat = dmat + jnp.dot(x, rmat[int(rev)], preferred_element_type=F32)
        first, last = (L - 1, 0) if rev else (0, L - 1)
        f_first = f_row[:, first:first + 1]
        b_col = dmat[:, first:first + 1] + f_first
        b_last = dmat[last:last + 1, first:first + 1] + f_first
        log_w = jnp.where(incl, dmat + i_row, -jnp.inf)
        log_inter = b_col + m
        m_t = jnp.maximum(log_inter, jnp.max(log_w, axis=-1, keepdims=True))
        w_intra = jnp.exp(log_w - m_t)
        w_inter = jnp.exp(log_inter - m_t)
        s = jnp.dot(q, kt, preferred_element_type=F32) * w_intra
        numden = (w_inter * jnp.dot(q, Caug.astype(BF16), preferred_element_type=F32)
                  + jnp.dot(s.astype(BF16), va, preferred_element_type=F32))
        inv = 1.0 / jnp.maximum(jnp.abs(numden[:, d:d + 1]), jnp.exp(-m_t))
        h_out = numden[:, 0:d] * inv
        log_ws = dmat[last:last + 1, :] + i_row
        m_new = jnp.maximum(b_last + m, jnp.max(log_ws, axis=-1, keepdims=True))
        decay = jnp.exp(b_last + m - m_new)
        kw = (kt.astype(F32) * jnp.exp(log_ws - m_new)).astype(BF16)
        Caug = decay * Caug + jnp.dot(kw, va, preferred_element_type=F32)
        return (Caug, m_new), h_out

    def scan_body(t, carries):
        out = []
        for hh in range(hp):
            for rev in (False, True):
                c = (nc - 1 - t) if rev else t
                carry, h_out = scan_chunk(c, carries[2 * hh + int(rev)], rev, hh)
                dst = hb if rev else hf
                dst[pl.ds(pl.multiple_of(c * L, L), L), hh * d:(hh + 1) * d] = h_out
                out.append(carry)
        return tuple(out)

    init = tuple((jnp.zeros((d, 2 * d), F32), jnp.zeros((1, 1), F32)) for _ in range(2 * hp))
    lax.fori_loop(0, nc, scan_body, init)

    slab = 4 * L

    def out_body(c, carry):
        base = pl.multiple_of(c * slab, slab)
        gate = jax.nn.sigmoid(og_ref[pl.ds(base, slab), :].astype(F32))
        for hh in range(hp):
            cols = slice(hh * d, (hh + 1) * d)
            h = hf[pl.ds(base, slab), cols] + hb[pl.ds(base, slab), cols]
            h = h * lax.rsqrt(jnp.mean(h * h, axis=-1, keepdims=True) + EPS) * ng_ref[:, cols]
            o_ref[pl.ds(base, slab), cols] = (h * gate[:, cols]).astype(o_ref.dtype)
        return carry

    lax.fori_loop(0, seq // slab, out_body, 0)


def _mlstm(proj, gates_row, gate_b, conv_w, conv_b, norm_g, batch, seq, n_heads, q_col, hp=2):
    T = proj.shape[0]
    d = HEAD_DIM
    width = n_heads * d
    w = hp * d
    c0 = q_col // w
    npair = n_heads // hp
    col = lambda off: (lambda b, h, *_: (b, c0 + off * npair + h))
    seq_blk = lambda off: pl.BlockSpec((seq, w), col(off))
    grid_spec = pltpu.PrefetchScalarGridSpec(
        num_scalar_prefetch=1,
        grid=(batch, npair),
        in_specs=[
            seq_blk(0), seq_blk(1), seq_blk(2), seq_blk(3),
            pl.BlockSpec((None, hp, 4, seq), lambda b, h, *_: (b, h, 0, 0)),
            pl.BlockSpec((CONV_WIDTH, w), lambda b, h, *_: (0, h)),
            pl.BlockSpec((CONV_WIDTH, w), lambda b, h, *_: (0, npair + h)),
            pl.BlockSpec((1, w), lambda b, h, *_: (0, h)),
            pl.BlockSpec((1, w), lambda b, h, *_: (0, npair + h)),
            pl.BlockSpec((1, w), lambda b, h, *_: (0, h)),
        ],
        out_specs=pl.BlockSpec((seq, w), lambda b, h, *_: (b, h)),
        scratch_shapes=[pltpu.VMEM((seq, w), BF16), pltpu.VMEM((w, seq), BF16), pltpu.VMEM((seq, 2 * w), BF16),
                        pltpu.VMEM((seq, w), F32), pltpu.VMEM((seq, w), F32),
                        pltpu.VMEM((2, 2 * MLSTM_CHUNK, MLSTM_CHUNK), BF16)],
    )
    return pl.pallas_call(
        functools.partial(_mlstm_kernel, seq=seq, n_heads=n_heads, hp=hp),
        grid_spec=grid_spec,
        out_shape=jax.ShapeDtypeStruct((T, width), BF16),
        compiler_params=_params("arbitrary", "arbitrary"),
        name="mlstm_bidir",
    )(gate_b.astype(F32), proj, proj, proj, proj, gates_row, conv_w, conv_w,
      conv_b.reshape(1, 2 * width), conv_b.reshape(1, 2 * width), norm_g.reshape(1, width))


def _diff_kernel(lam_ref, ng_ref, q_ref, k_ref, v_ref, o_ref, vt_sc, qt_sc, acc_sc, *, tq, tk, seq, lam_init):
    d = HEAD_DIM
    qs = DIFF_QUERY_STRIP

    @pl.when(pl.program_id(2) == 0)
    def _():
        for c in range(seq // tk):
            vt_sc[0:d, c * tk:(c + 1) * tk] = v_ref[c * tk:(c + 1) * tk, :].astype(F32).T.astype(BF16)
        vt_sc[d:, :] = jnp.ones((vt_sc.shape[0] - d, seq), BF16)

    qt = q_ref[...].astype(F32).T
    row = lax.broadcasted_iota(I32, (d, qs), 0)
    for s in range(tq // qs):
        qts = qt[:, s * qs:(s + 1) * qs]
        qt_sc[:, 2 * s * qs:(2 * s + 1) * qs] = jnp.where(row < d // 2, qts, 0.0).astype(BF16)
        qt_sc[:, (2 * s + 1) * qs:(2 * s + 2) * qs] = jnp.where(row >= d // 2, qts, 0.0).astype(BF16)

    lam = lam_ref[...]
    lam_full = (jnp.exp(jnp.sum(lam[0:1] * lam[1:2], axis=-1, keepdims=True))
                - jnp.exp(jnp.sum(lam[2:3] * lam[3:4], axis=-1, keepdims=True)) + lam_init)

    n_strips = tq // qs
    nk = seq // tk
    m = jnp.full((1, 2 * tq), -jnp.inf, F32)
    acc_sc[...] = jnp.zeros(acc_sc.shape, F32)
    scores = lambda j: jnp.dot(k_ref[j * tk:(j + 1) * tk, :], qt_sc[...], preferred_element_type=F32)
    st = scores(0)
    for j in range(nk):
        st_next = scores(j + 1) if j + 1 < nk else None
        m_new = jnp.maximum(m, jnp.max(st, axis=0, keepdims=True))
        p = jnp.exp2(st - m_new).astype(BF16)
        acc_sc[...] = jnp.exp2(m - m_new) * acc_sc[...] + jnp.dot(vt_sc[:, j * tk:(j + 1) * tk], p,
                                                                preferred_element_type=F32)
        m = m_new
        st = st_next
    for s in range(n_strips):
        acc = acc_sc[:, 2 * s * qs:(2 * s + 2) * qs]
        num = acc[0:d] / acc[d:d + 1]
        ot = num[:, 0:qs] - lam_full * num[:, qs:2 * qs]
        o = ot.T
        o = o * lax.rsqrt(jnp.mean(o * o, axis=-1, keepdims=True) + EPS) * ng_ref[...] * (1.0 - lam_init)
        o_ref[s * qs:(s + 1) * qs, :] = o.astype(o_ref.dtype)


def _diff_attention(proj, lam, norm_g, batch, seq, n_heads, q_col, layer_idx, tq=1024, tk=512):
    T = proj.shape[0]
    d = HEAD_DIM
    nq = seq // tq
    c0 = q_col // d
    lam_init = 0.8 - 0.6 * math.exp(-0.3 * layer_idx)
    return pl.pallas_call(
        functools.partial(_diff_kernel, tq=tq, tk=tk, seq=seq, lam_init=lam_init),
        grid=(batch, n_heads, nq),
        in_specs=[
            pl.BlockSpec(lam.shape, lambda b, h, i: (0, 0)),
            pl.BlockSpec((1, d), lambda b, h, i: (0, 0)),
            pl.BlockSpec((tq, d), lambda b, h, i: (b * nq + i, c0 + h)),
            pl.BlockSpec((seq, d), lambda b, h, i: (b, c0 + n_heads + h)),
            pl.BlockSpec((seq, d), lambda b, h, i: (b, c0 + 2 * n_heads + h)),
        ],
        out_specs=pl.BlockSpec((tq, d), lambda b, h, i: (b * nq + i, h)),
        out_shape=jax.ShapeDtypeStruct((T, n_heads * d), BF16),
        scratch_shapes=[pltpu.VMEM((d + BF16_SUBLANES, seq), BF16), pltpu.VMEM((d, 2 * tq), BF16),
                        pltpu.VMEM((d + BF16_SUBLANES, 2 * tq), F32)],
        compiler_params=_params("arbitrary", "arbitrary", "arbitrary"),
        name="diff_attention",
    )(lam.astype(F32), norm_g.reshape(1, d).astype(F32), proj, proj, proj)


def _outproj_kernel(a_ref, b_ref, c_ref, w_ref, x_ref, g_ref, o_ref):
    ka, kb = a_ref.shape[1], b_ref.shape[1]
    acc = jnp.dot(a_ref[...], w_ref[0:ka, :], preferred_element_type=F32)
    acc = acc + jnp.dot(b_ref[...], w_ref[ka:ka + kb, :], preferred_element_type=F32)
    acc = acc + jnp.dot(c_ref[...], w_ref[ka + kb:, :], preferred_element_type=F32)
    o_ref[...] = x_ref[...] + g_ref[0] * acc


def _out_projection(a, b, c, w, x, gate, seq, tm=1024, tn=512):
    T, D = x.shape
    B = gate.shape[0]
    per_b = seq // tm
    K = w.shape[0]
    return pl.pallas_call(
        _outproj_kernel,
        grid=(T // tm, D // tn),
        in_specs=[
            pl.BlockSpec((tm, a.shape[1]), lambda i, j: (i, 0)),
            pl.BlockSpec((tm, b.shape[1]), lambda i, j: (i, 0)),
            pl.BlockSpec((tm, c.shape[1]), lambda i, j: (i, 0)),
            pl.BlockSpec((K, tn), lambda i, j: (0, j)),
            pl.BlockSpec((tm, tn), lambda i, j: (i, j)),
            pl.BlockSpec((1, 1, tn), lambda i, j: (i // per_b, 0, j)),
        ],
        out_specs=pl.BlockSpec((tm, tn), lambda i, j: (i, j)),
        out_shape=jax.ShapeDtypeStruct((T, D), F32),
        compiler_params=_params("arbitrary", "arbitrary"),
        name="out_projection_residual",
    )(a, b, c, w, x, gate.reshape(B, 1, D))


def _shared_gu_kernel(h_ref, sg_ref, su_ref, o_ref):
    h = h_ref[...]
    g = jnp.dot(h, sg_ref[...], preferred_element_type=F32)
    u = jnp.dot(h, su_ref[...], preferred_element_type=F32)
    o_ref[...] = (_silu(g) * u).astype(o_ref.dtype)


def _shared_gate_up(h, sg, su, tm=512):
    T, D = h.shape
    Fs = sg.shape[1]
    return pl.pallas_call(
        _shared_gu_kernel,
        grid=(T // tm,),
        in_specs=[pl.BlockSpec((tm, D), lambda i: (i, 0)),
                  pl.BlockSpec((D, Fs), lambda i: (0, 0)),
                  pl.BlockSpec((D, Fs), lambda i: (0, 0))],
        out_specs=pl.BlockSpec((tm, Fs), lambda i: (i, 0)),
        out_shape=jax.ShapeDtypeStruct((T, Fs), BF16),
        compiler_params=_params("arbitrary"),
        name="shared_gate_up",
    )(h, sg, su)


def _row_gather_start(idx_ref, n_rows, src_hbm, dst, sem):
    def body(r, carry):
        t = idx_ref[0, 0, r]
        pltpu.make_async_copy(src_hbm.at[pl.ds(t, 1), :], dst.at[pl.ds(r, 1), :], sem).start()
        return carry
    lax.fori_loop(0, n_rows, body, 0, unroll=8)


def _expert_kernel(be_ref, nact_ref, idx0_ref, idxn_ref, h_hbm, wg_ref, wu_ref, wd_ref, y_ref, xbuf, sem, *, tm):
    i = pl.program_id(0)
    last = pl.num_programs(0) - 1
    nact = nact_ref[0]

    def gather_wait(slot):
        pltpu.make_async_copy(h_hbm.at[pl.ds(0, tm), :], xbuf.at[slot], sem.at[slot]).wait()

    @pl.when(i == 0)
    def _():
        _row_gather_start(idx0_ref, tm, h_hbm, xbuf.at[0], sem.at[0])

    def compute(slot):
        def start_rows(lo, hi):
            for r in range(lo, hi):
                t = idxn_ref[0, 0, r]
                pltpu.make_async_copy(h_hbm.at[pl.ds(t, 1), :], xbuf.at[1 - slot, pl.ds(r, 1), :],
                                      sem.at[1 - slot]).start()

        gather_wait(slot)
        x = xbuf[slot].astype(BF16)
        g = jnp.dot(x, wg_ref[0].astype(BF16), preferred_element_type=F32)
        start_rows(0, tm // 2)
        u = jnp.dot(x, wu_ref[0].astype(BF16), preferred_element_type=F32)
        start_rows(tm // 2, tm)
        hid = (_silu(g) * u).astype(BF16)
        y_ref[...] = jnp.dot(hid, wd_ref[0].astype(BF16), preferred_element_type=F32)

    for slot in (0, 1):
        pl.when(jnp.logical_and(i < nact, i % 2 == slot))(functools.partial(compute, slot))

    @pl.when(i == nact)
    def _():
        gather_wait(i % 2)

    @pl.when(jnp.logical_and(i == last, nact == last + 1))
    def _():
        gather_wait((i + 1) % 2)

    @pl.when(i >= nact)
    def _():
        y_ref[...] = jnp.zeros(y_ref.shape, y_ref.dtype)


def _routed_experts(h32, row_t, blk_e, nact, wg, wu, wd, layer, tm):
    T, D = h32.shape
    _, E, _, Fe = wg.shape
    nblk = row_t.shape[0] // tm
    idx = row_t.reshape(nblk, 1, tm)
    grid_spec = pltpu.PrefetchScalarGridSpec(
        num_scalar_prefetch=2,
        grid=(nblk,),
        in_specs=[
            pl.BlockSpec((1, 1, tm), lambda i, be, na: (0, 0, 0), memory_space=pltpu.SMEM),
            pl.BlockSpec((1, 1, tm), lambda i, be, na: (jnp.minimum(i + 1, nblk - 1), 0, 0),
                         memory_space=pltpu.SMEM),
            pl.BlockSpec(memory_space=pl.ANY),
            pl.BlockSpec((None, 1, D, Fe), lambda i, be, na: (layer, be[i], 0, 0), pipeline_mode=pl.Buffered(1)),
            pl.BlockSpec((None, 1, D, Fe), lambda i, be, na: (layer, be[i], 0, 0), pipeline_mode=pl.Buffered(1)),
            pl.BlockSpec((None, 1, Fe, D), lambda i, be, na: (layer, be[i], 0, 0), pipeline_mode=pl.Buffered(1)),
        ],
        out_specs=pl.BlockSpec((tm, D), lambda i, be, na: (i, 0)),
        scratch_shapes=[pltpu.VMEM((2, tm, D), F32), pltpu.SemaphoreType.DMA((2,))],
    )
    return pl.pallas_call(
        functools.partial(_expert_kernel, tm=tm),
        grid_spec=grid_spec,
        out_shape=jax.ShapeDtypeStruct((nblk * tm, D), F32),
        compiler_params=_params("arbitrary"),
        name="routed_experts",
    )(blk_e, nact, idx, idx, h32, wg, wu, wd)


def _combine_kernel(pos0_ref, posn_ref, w_ref, hs_ref, sd_ref, x_ref, g_ref, y_hbm, ng_ref, *refs, tc, top_k,
                    next_layer):
    if next_layer:
        sc_ref, sh_ref, ws_ref, o_ref, h_ref, st_ref, buf, sem = refs
    else:
        o_ref, buf, sem = refs
    i = pl.program_id(0)
    n = pl.num_programs(0)
    slot = i % 2

    def start(pos_ref, s):
        for k in range(top_k):
            def body(r, carry):
                p = pos_ref[0, 0, k * tc + r]
                pltpu.make_async_copy(y_hbm.at[pl.ds(p, 1), :], buf.at[s, k, pl.ds(r, 1), :], sem.at[s]).start()
                return carry
            lax.fori_loop(0, tc, body, 0, unroll=8)

    @pl.when(i == 0)
    def _():
        start(pos0_ref, 0)

    def wait_rows(s):
        for k in range(top_k):
            pltpu.make_async_copy(y_hbm.at[pl.ds(0, tc), :], buf.at[s, k], sem.at[s]).wait()

    def tile(s):
        def start_next(k):
            for r in range(tc):
                p = posn_ref[0, 0, k * tc + r]
                pltpu.make_async_copy(y_hbm.at[pl.ds(p, 1), :], buf.at[1 - s, k, pl.ds(r, 1), :],
                                      sem.at[1 - s]).start()

        wait_rows(s)
        w = w_ref[...]
        routed = buf[s, 0] * w[:, 0:1]
        start_next(0)
        for k in range(1, top_k):
            routed = routed + buf[s, k] * w[:, k:k + 1]
            start_next(k)
        shared = jnp.dot(hs_ref[...], sd_ref[...], preferred_element_type=F32)
        xn = x_ref[...] + g_ref[0] * (routed + shared)
        normed = xn * lax.rsqrt(jnp.mean(xn * xn, axis=-1, keepdims=True) + EPS) * ng_ref[...]
        if next_layer:
            o_ref[...] = xn
            h = normed * (1.0 + sc_ref[0]) + sh_ref[0]
            h_ref[...] = h.astype(BF16)
            st_ref[...] = jnp.dot(h, ws_ref[...], preferred_element_type=F32, precision=lax.Precision.HIGHEST).T
        else:
            o_ref[...] = normed

    for s in (0, 1):
        pl.when(slot == s)(functools.partial(tile, s))

    @pl.when(i == n - 1)
    def _():
        wait_rows(1 - slot)


def _moe_combine(y, pos, w, hs, sd, x, gate, seq, norm_g, next_mod=None, w_small=None, tc=128):
    T, D = x.shape
    B = gate.shape[0]
    K = pos.shape[0]
    Fs = hs.shape[1]
    nt = T // tc
    per_b = seq // tc
    pos_tiles = pos.reshape(K, nt, tc).transpose(1, 0, 2).reshape(nt, 1, K * tc)
    w = w.T
    row = lambda i: (i, 0)
    per_batch = lambda i: (i // per_b, 0, 0)
    in_specs = [
        pl.BlockSpec((1, 1, K * tc), lambda i: (0, 0, 0), memory_space=pltpu.SMEM),
        pl.BlockSpec((1, 1, K * tc), lambda i: (jnp.minimum(i + 1, nt - 1), 0, 0), memory_space=pltpu.SMEM),
        pl.BlockSpec((tc, K), row),
        pl.BlockSpec((tc, Fs), row),
        pl.BlockSpec((Fs, D), lambda i: (0, 0)),
        pl.BlockSpec((tc, D), row),
        pl.BlockSpec((1, 1, D), per_batch),
        pl.BlockSpec(memory_space=pl.ANY),
        pl.BlockSpec((1, D), lambda i: (0, 0)),
    ]
    args = [pos_tiles, pos_tiles, w, hs, sd, x, gate.reshape(B, 1, D), y, norm_g.reshape(1, D)]
    out_specs = [pl.BlockSpec((tc, D), row)]
    out_shape = [jax.ShapeDtypeStruct((T, D), F32)]
    if next_mod is not None:
        ns = w_small.shape[1]
        in_specs += [pl.BlockSpec((1, 1, D), per_batch), pl.BlockSpec((1, 1, D), per_batch),
                     pl.BlockSpec((D, ns), lambda i: (0, 0))]
        args += [next_mod[0].reshape(B, 1, D), next_mod[1].reshape(B, 1, D), w_small]
        out_specs += [pl.BlockSpec((tc, D), row), pl.BlockSpec((ns, tc), lambda i: (0, i))]
        out_shape += [jax.ShapeDtypeStruct((T, D), BF16), jax.ShapeDtypeStruct((ns, T), F32)]
    out = pl.pallas_call(
        functools.partial(_combine_kernel, tc=tc, top_k=K, next_layer=next_mod is not None),
        grid=(nt,),
        in_specs=in_specs,
        out_specs=out_specs,
        out_shape=out_shape,
        scratch_shapes=[pltpu.VMEM((2, K, tc, D), F32), pltpu.SemaphoreType.DMA((2,))],
        compiler_params=_params("arbitrary"),
        name="moe_combine_residual",
    )(*args)
    return out if next_mod is not None else out[0]


def _route_kernel(lt_ref, b_ref, u_ref, eid_ref, rnk_ref, w_ref, cnt_ref, carry_sc):
    G, P, K = N_GROUPS, N_EXPERTS // N_GROUPS, TOP_K
    E = G * P
    tr = lt_ref.shape[1]

    @pl.when(pl.program_id(0) == 0)
    def _():
        carry_sc[...] = jnp.zeros(carry_sc.shape, F32)

    one = lambda cond: jnp.where(cond, 1.0, 0.0)
    gidx = lax.broadcasted_iota(I32, (G, tr), 0)
    rows = lambda a, g: jnp.broadcast_to(a[g:g + 1, :], (G, tr))
    score = [jax.nn.sigmoid(lt_ref[j * G:(j + 1) * G, :]) for j in range(P)]
    biased = [score[j] + b_ref[j * G:(j + 1) * G, 0:1] for j in range(P)]

    top1, top2 = biased[0], jnp.full((G, tr), -jnp.inf, F32)
    for j in range(1, P):
        top2 = jnp.maximum(top2, jnp.minimum(top1, biased[j]))
        top1 = jnp.maximum(top1, biased[j])
    gscore = top1 + top2

    beaten = jnp.zeros((G, tr), F32)
    for c in range(G):
        vc = rows(gscore, c)
        beaten = beaten + one(vc > gscore) + jnp.where(gidx > c, one(vc == gscore), 0.0)
    keep = beaten < TOPK_GROUPS
    masked = [jnp.where(keep, biased[j], -jnp.inf) for j in range(P)]

    rank = [jnp.zeros((G, tr), F32) for _ in range(P)]
    for jc in range(P):
        for gc in range(G):
            vc = rows(masked[jc], gc)
            for j in range(P):
                lower_id = (gidx >= gc) if j > jc else (gidx > gc)
                rank[j] = rank[j] + one(vc > masked[j]) + jnp.where(lower_id, one(vc == masked[j]), 0.0)
    sel = [jnp.where(keep, one(rank[j] < K), 0.0) for j in range(P)]

    picked = sel[0] * score[0]
    for j in range(1, P):
        picked = picked + sel[j] * score[j]
    total = jnp.sum(picked, axis=0, keepdims=True)
    weight = [sel[j] * score[j] / total * ROUTED_SCALE for j in range(P)]

    sel_all = jnp.concatenate(sel, axis=0).astype(BF16)
    earlier = jnp.dot(sel_all, u_ref[...], preferred_element_type=F32)
    carry = carry_sc[...]
    place = [earlier[j * G:(j + 1) * G, :] + carry[j * G:(j + 1) * G, 0:1] for j in range(P)]
    carry_sc[0:E, :] = carry[0:E, :] + jnp.dot(sel_all, jnp.ones((tr, LANES), BF16), preferred_element_type=F32)
    cnt_ref[...] = carry_sc[...]

    eid = [(gidx * P + j).astype(F32) for j in range(P)]
    pick = lambda vals, hit: jnp.sum(sum(hit[j] * vals[j] for j in range(P)), axis=0, keepdims=True)
    out_e, out_r, out_w = [], [], []
    for k in range(K):
        hit = [sel[j] * one(rank[j] == k) for j in range(P)]
        out_e.append(pick(eid, hit))
        out_r.append(pick(place, hit))
        out_w.append(pick(weight, hit))
    fill = [jnp.zeros((eid_ref.shape[0] - K, tr), F32)]
    eid_ref[...] = jnp.concatenate(out_e + fill, axis=0).astype(I32)
    rnk_ref[...] = jnp.concatenate(out_r + fill, axis=0).astype(I32)
    w_ref[...] = jnp.concatenate(out_w + fill, axis=0)


def _route(logits_t, bias_col, tr=512):
    T = logits_t.shape[1]
    slots = 8
    strictly_earlier = jnp.triu(jnp.ones((tr, tr), BF16), 1)
    tile = lambda i: (0, i)
    eid, place, w, cnt = pl.pallas_call(
        _route_kernel,
        grid=(T // tr,),
        in_specs=[pl.BlockSpec((LANES, tr), tile),
                  pl.BlockSpec((LANES, 1), lambda i: (0, 0)),
                  pl.BlockSpec((tr, tr), lambda i: (0, 0))],
        out_specs=[pl.BlockSpec((slots, tr), tile), pl.BlockSpec((slots, tr), tile), pl.BlockSpec((slots, tr), tile),
                   pl.BlockSpec((LANES, LANES), lambda i: (0, 0))],
        out_shape=[jax.ShapeDtypeStruct((slots, T), I32), jax.ShapeDtypeStruct((slots, T), I32),
                   jax.ShapeDtypeStruct((slots, T), F32), jax.ShapeDtypeStruct((LANES, LANES), F32)],
        scratch_shapes=[pltpu.VMEM((LANES, LANES), F32)],
        compiler_params=_params("arbitrary"),
        name="moe_routing",
    )(logits_t, bias_col, strictly_earlier)
    return eid[:TOP_K], place[:TOP_K], w[:TOP_K], cnt[:N_EXPERTS, 0].astype(I32)


def _router_row_order(a):
    G, P = N_GROUPS, N_EXPERTS // N_GROUPS
    return a.reshape(a.shape[:-1] + (G, P)).swapaxes(-1, -2).reshape(a.shape[:-1] + (G * P,))


def _dispatch_plan(eid, place, counts_rows, tm):
    K, T = eid.shape
    E, G, P = N_EXPERTS, N_GROUPS, N_EXPERTS // N_GROUPS
    counts = counts_rows.reshape(P, G).T.reshape(E)
    padded = (counts + tm - 1) // tm * tm
    pad_end = jnp.cumsum(padded)
    pad_start = pad_end - padded
    ids = jnp.arange(E, dtype=I32)
    dest = place + jnp.sum(jnp.where(eid[..., None] == ids, pad_start, 0), axis=-1)
    nblk = K * T // tm + E
    tok = jnp.broadcast_to(jnp.arange(T, dtype=I32)[None, :], (K, T))
    row_t = jnp.zeros((nblk * tm,), I32).at[dest.reshape(-1)].set(tok.reshape(-1), unique_indices=True)
    nact = (pad_end[-1] // tm).astype(I32)
    blk = jnp.minimum(jnp.arange(nblk, dtype=I32), nact - 1)
    blk_e = jnp.sum((pad_end[None, :] <= (blk * tm)[:, None]).astype(I32), axis=1)
    return row_t, blk_e, nact.reshape(1), dest


def kernel(x, c, ada_w, ada_b, ada_table, norm1_g, w_in, swa_sink, mlstm_conv_w, mlstm_conv_b, mlstm_gate_b,
           mlstm_norm_g, diff_lambda, diff_norm_g, w_out, norm2_g, router_w, router_b, exp_gate, exp_up,
           exp_down, sh_gate, sh_up, sh_down, final_g):
    B, S, D = x.shape
    depth = w_in.shape[0]
    T = B * S
    d = HEAD_DIM
    n_swa_q = swa_sink.shape[1]
    n_swa_kv = n_swa_q // SWA_GROUP
    n_ml = mlstm_gate_b.shape[1] // 4
    n_diff = (D // d) - n_swa_q - n_ml
    swa_q_w, swa_kv_w, ml_w, diff_w = n_swa_q * d, n_swa_kv * d, n_ml * d, n_diff * d
    gate_w = 4 * n_ml
    gate_off = swa_q_w + 2 * swa_kv_w + 4 * ml_w
    k_a, v_a = swa_q_w, swa_q_w + swa_kv_w
    q_m = swa_q_w + 2 * swa_kv_w
    q_c = gate_off
    tn = 512
    diff_q_scale = (d // 2) ** -0.5 * math.log2(math.e)
    tiles = ((swa_q_w + swa_kv_w) // tn, q_c // tn, (q_c + diff_w) // tn, (q_c + 2 * diff_w) // tn, diff_q_scale)

    cos_a, sin_a = _rope_tables(S, d // 2, tn)
    cos_c, sin_c = _rope_tables(S, d // 4, tn)
    rope_tabs = (cos_a, sin_a, cos_c, sin_c)

    mod = _modulation(c, ada_w, ada_b, ada_table)
    gate_cols = lambda l: jnp.pad(w_in[l, :, gate_off:gate_off + gate_w], ((0, 0), (0, LANES - gate_w)))
    xt = x.reshape(T, D)
    h, gates = _norm_mod(xt, norm1_g[0], mod[0, :, 1], mod[0, :, 0], gate_cols(0), S, emit_f32=False)
    for l in range(depth):
        sh_a, sc_a, g_a, sh_f, sc_f, g_f = [mod[l, :, i] for i in range(ADA_CHUNKS)]
        w_main = jnp.concatenate([w_in[l, :, :gate_off], w_in[l, :, gate_off + gate_w:]], axis=1).astype(BF16)
        proj = _in_projection(h, w_main, rope_tabs, S, tiles, tn=tn)
        out_a = _swa_attention(proj, swa_sink[l], B, S, n_swa_kv, 0, k_a, v_a)
        g4 = gates[:gate_w].reshape(4, n_ml, B, S)
        out_b = _mlstm(proj, g4.transpose(2, 1, 0, 3), mlstm_gate_b[l],
                       mlstm_conv_w[l], mlstm_conv_b[l], mlstm_norm_g[l], B, S, n_ml, q_m)
        out_c = _diff_attention(proj, diff_lambda[l], diff_norm_g[l], B, S, n_diff, q_c, l)
        xt = _out_projection(out_a, out_b, out_c, w_out[l].astype(BF16), xt, g_a, S)

        w_router = jnp.pad(_router_row_order(router_w[l]), ((0, 0), (0, LANES - N_EXPERTS)))
        b_router = jnp.pad(_router_row_order(router_b[l].astype(F32)), (0, LANES - N_EXPERTS)).reshape(LANES, 1)
        h, h32, logits_t = _norm_mod(xt, norm2_g[l], sc_f, sh_f, w_router, S, emit_f32=True)
        eid, place, wts, counts = _route(logits_t, b_router)
        tm = 256
        row_t, blk_e, nact, pos = _dispatch_plan(eid, place, counts, tm)
        y = _routed_experts(h32, row_t, blk_e, nact, exp_gate, exp_up, exp_down, l, tm)
        hs = _shared_gate_up(h, sh_gate[l].astype(BF16), sh_up[l].astype(BF16))
        sd = sh_down[l].astype(BF16)
        if l + 1 < depth:
            xt, h, gates = _moe_combine(y, pos, wts, hs, sd, xt, g_f, S, norm1_g[l + 1],
                                        next_mod=(mod[l + 1, :, 1], mod[l + 1, :, 0]), w_small=gate_cols(l + 1))
        else:
            xt = _moe_combine(y, pos, wts, hs, sd, xt, g_f, S, final_g)
    return xt.reshape(B, S, D)
```
